```python
import math
import jax, jax.numpy as jnp
from jax import lax
import numpy as np

D_MODEL = 4096
BATCH = 4
SEQ = 2048
DEPTH = 1
DEC_BATCH = 2
DEC_SEQ = 4096
PAST_LEN = 128

GRID_W = 64
WIN_R = 8
WIN_C = 16
NA_HEAD_DIM = 128
NA_WIDTH = D_MODEL // 2
NA_HEADS = NA_WIDTH // NA_HEAD_DIM
SGU_WIDTH = D_MODEL // 2
SGU_GROUPS = 8
SGU_CHUNK = 128
N_EXPERTS = 32
TOP_K = 4
D_FF = D_MODEL
SWIGLU_ALPHA = 1.702
SWIGLU_LIMIT = 7.0
MOE_BLOCK = 256
PLE_DIM = 256
LN_EPS = 1e-5
DN_ALPHA = (2.0 * DEPTH) ** 0.25
DN_BETA = (8.0 * DEPTH) ** -0.25
IN_COLS = 3 * NA_WIDTH + 2 * SGU_WIDTH + 2 * D_MODEL

kernel_name = "hybrid_na_sgu_moe_encoder"


def _layer_norm(x, g, b):
    xf = x.astype(jnp.float32)
    mu = jnp.mean(xf, axis=-1, keepdims=True)
    xc = xf - mu
    var = jnp.mean(xc * xc, axis=-1, keepdims=True)
    y = xc * lax.rsqrt(var + LN_EPS) * g.astype(jnp.float32) + b.astype(jnp.float32)
    return y.astype(x.dtype)


def _neighbourhood_attention(q, k, v, rpb):
    B, T, H, Dh = q.shape
    rows = T // GRID_W
    wr = min(WIN_R, rows)
    n_keys = wr * WIN_C
    scale = Dh ** -0.5
    cols = jnp.arange(GRID_W)
    col_start = jnp.clip(cols - WIN_C // 2, 0, GRID_W - WIN_C)
    j = jnp.arange(n_keys)
    j_row = j // WIN_C
    j_col = j % WIN_C
    key_col = col_start[:, None] + j_col[None, :]
    dc_idx = key_col - cols[:, None] + (WIN_C - 1)

    def row_fn(r):
        row_start = jnp.clip(r - wr // 2, 0, rows - wr)
        key_row = row_start + j_row
        dr_idx = (key_row - r + (WIN_R - 1))[None, :]
        idx = key_row[None, :] * GRID_W + key_col
        bias = rpb[:, dr_idx, dc_idx].astype(jnp.float32)
        q_r = lax.dynamic_slice_in_dim(q, r * GRID_W, GRID_W, axis=1)
        k_g = jnp.take(k, idx, axis=1)
        v_g = jnp.take(v, idx, axis=1)
        s = jnp.einsum('bqhd,bqkhd->bhqk', q_r, k_g).astype(jnp.float32) * scale + bias[None]
        pr = jax.nn.softmax(s, axis=-1).astype(v.dtype)
        return jnp.einsum('bhqk,bqkhd->bqhd', pr, v_g)

    out = lax.map(row_fn, jnp.arange(rows))
    return jnp.transpose(out, (1, 0, 2, 3, 4)).reshape(B, T, H * Dh)


def _spatial_gating(u, vs, ln_g, ln_b, w_s, b_s):
    B, T, _ = u.shape
    u = jax.nn.gelu(u, approximate=False)
    vs = _layer_norm(jax.nn.gelu(vs, approximate=False), ln_g, ln_b)
    vc = vs.reshape(B, T // SGU_CHUNK, SGU_CHUNK, SGU_GROUPS, SGU_WIDTH // SGU_GROUPS)
    s = jnp.einsum('gpq,bnqgc->bnpgc', w_s, vc) + b_s.T[None, None, :, :, None]
    return u * s.reshape(B, T, SGU_WIDTH)


def _token_mixer(x, w_in, b_in, rpb, sgu_ln_g, sgu_ln_b, sgu_w, sgu_b, w_pa, w_pb, w_o):
    B, T, _ = x.shape
    z = jnp.dot(x, w_in) + b_in
    splits = [NA_WIDTH, 2 * NA_WIDTH, 3 * NA_WIDTH, 3 * NA_WIDTH + SGU_WIDTH,
              3 * NA_WIDTH + 2 * SGU_WIDTH, 3 * NA_WIDTH + 2 * SGU_WIDTH + D_MODEL]
    q, k, v, u, vs, g_a, g_b = jnp.split(z, splits, axis=-1)
    hs = (B, T, NA_HEADS, NA_HEAD_DIM)
    y_a = jnp.dot(_neighbourhood_attention(q.reshape(hs), k.reshape(hs), v.reshape(hs), rpb), w_pa)
    y_b = jnp.dot(_spatial_gating(u, vs, sgu_ln_g, sgu_ln_b, sgu_w, sgu_b), w_pb)
    merged = jax.nn.sigmoid(g_a) * y_a + jax.nn.sigmoid(g_b) * y_b
    return jnp.dot(merged, w_o)


def _moe(h, w_router, b_router, w_gu, b_gu, w_down, b_down, layer):
    B, T, D = h.shape
    n_tok = B * T
    xf = h.reshape(n_tok, D)
    logits = jnp.dot(xf, w_router).astype(jnp.float32) + b_router.astype(jnp.float32)
    top_val, top_idx = lax.top_k(logits, TOP_K)
    gate = jax.nn.softmax(top_val, axis=-1).astype(h.dtype)
    n_assign = n_tok * TOP_K
    flat_e = top_idx.reshape(n_assign).astype(jnp.int32)
    flat_tok = jnp.arange(n_assign, dtype=jnp.int32) // TOP_K
    flat_w = gate.reshape(n_assign)
    order = jnp.argsort(flat_e, stable=True)
    sorted_e = flat_e[order]
    counts = jnp.bincount(flat_e, length=N_EXPERTS).astype(jnp.int32)
    starts = jnp.cumsum(counts) - counts
    padded = (counts + MOE_BLOCK - 1) // MOE_BLOCK * MOE_BLOCK
    pad_ends = jnp.cumsum(padded)
    pad_starts = pad_ends - padded
    dest = pad_starts[sorted_e] + jnp.arange(n_assign, dtype=jnp.int32) - starts[sorted_e]
    n_blocks = -(-n_assign // MOE_BLOCK) + N_EXPERTS
    n_slots = n_blocks * MOE_BLOCK
    slot_tok = jnp.full((n_slots,), n_tok, jnp.int32).at[dest].set(flat_tok[order])
    slot_w = jnp.zeros((n_slots,), h.dtype).at[dest].set(flat_w[order])
    block_start = jnp.arange(n_blocks, dtype=jnp.int32) * MOE_BLOCK
    block_e = jnp.minimum(jnp.sum(block_start[:, None] >= pad_ends[None, :], axis=1), N_EXPERTS - 1)
    x_pad = jnp.concatenate([xf, jnp.zeros((1, D), xf.dtype)], axis=0)
    xs = jnp.take(x_pad, slot_tok, axis=0).reshape(n_blocks, MOE_BLOCK, D)

    def block_fn(args):
        xb, e = args
        gu = jnp.dot(xb, w_gu[layer, e]) + b_gu[layer, e]
        g = jnp.minimum(gu[:, :D_FF], SWIGLU_LIMIT)
        up = jnp.clip(gu[:, D_FF:], -SWIGLU_LIMIT, SWIGLU_LIMIT)
        act = (up + 1.0) * (g * jax.nn.sigmoid(g * SWIGLU_ALPHA))
        return jnp.dot(act, w_down[layer, e]) + b_down[layer, e]

    ys = lax.map(block_fn, (xs, block_e)).reshape(n_slots, D)
    out = jax.ops.segment_sum(ys * slot_w[:, None], slot_tok, num_segments=n_tok + 1)[:n_tok]
    return out.reshape(B, T, D)


def _encode(x, p, ln_emb_g, ln_emb_b, w_in, b_in, rpb, sgu_ln_g, sgu_ln_b, sgu_w, sgu_b,
            w_pa, w_pb, w_o, ln1_g, ln1_b, w_router, b_router, w_gu, b_gu, w_down, b_down,
            w_pg, w_ple, ln2_g, ln2_b):
    x = _layer_norm(x, ln_emb_g, ln_emb_b)
    for l in range(DEPTH):
        m = _token_mixer(x, w_in[l], b_in[l], rpb[l], sgu_ln_g[l], sgu_ln_b[l], sgu_w[l], sgu_b[l],
                         w_pa[l], w_pb[l], w_o[l])
        h = _layer_norm(DN_ALPHA * x + m, ln1_g[l], ln1_b[l])
        f = _moe(h, w_router[l], b_router[l], w_gu, b_gu, w_down, b_down, l)
        e = jax.nn.sigmoid(jnp.dot(h, w_pg[l])) * jnp.dot(p[l], w_ple[l])
        x = _layer_norm(DN_ALPHA * h + f + e, ln2_g[l], ln2_b[l])
    return x


def setup_inputs(seed: int = 0) -> dict:
    key = jax.random.key(seed)
    ks = jax.random.split(key, 32)

    def nrm(k, shape, scale):
        return jax.random.normal(k, shape, jnp.float32) * scale

    C = SGU_CHUNK
    return {
        "x_prompt": nrm(ks[0], (BATCH, SEQ, D_MODEL), 1.0),
        "x_sample": nrm(ks[1], (DEC_BATCH, DEC_SEQ, D_MODEL), 1.0),
        "p_prompt": nrm(ks[2], (DEPTH, BATCH, SEQ, PLE_DIM), 1.0),
        "p_sample": nrm(ks[3], (DEPTH, DEC_BATCH, DEC_SEQ, PLE_DIM), 1.0),
        "ln_emb_g": 1.0 + nrm(ks[4], (D_MODEL,), 0.02),
        "ln_emb_b": nrm(ks[5], (D_MODEL,), 0.02),
        "w_in": nrm(ks[6], (DEPTH, D_MODEL, IN_COLS), D_MODEL ** -0.5),
        "b_in": nrm(ks[7], (DEPTH, IN_COLS), 0.02),
        "rpb": nrm(ks[8], (DEPTH, NA_HEADS, 2 * WIN_R - 1, 2 * WIN_C - 1), 0.1),
        "sgu_ln_g": 1.0 + nrm(ks[9], (DEPTH, SGU_WIDTH), 0.02),
        "sgu_ln_b": nrm(ks[10], (DEPTH, SGU_WIDTH), 0.02),
        "sgu_w": nrm(ks[11], (DEPTH, SGU_GROUPS, C, C), C ** -0.5),
        "sgu_b": 1.0 + nrm(ks[12], (DEPTH, SGU_GROUPS, C), 0.01),
        "w_pa": nrm(ks[13], (DEPTH, NA_WIDTH, D_MODEL), NA_WIDTH ** -0.5),
        "w_pb": nrm(ks[14], (DEPTH, SGU_WIDTH, D_MODEL), SGU_WIDTH ** -0.5),
        "w_o": nrm(ks[15], (DEPTH, D_MODEL, D_MODEL), DN_BETA * D_MODEL ** -0.5),
        "ln1_g": 1.0 + nrm(ks[16], (DEPTH, D_MODEL), 0.02),
        "ln1_b": nrm(ks[17], (DEPTH, D_MODEL), 0.02),
        "w_router": nrm(ks[18], (DEPTH, D_MODEL, N_EXPERTS), D_MODEL ** -0.5),
        "b_router": nrm(ks[19], (DEPTH, N_EXPERTS), 0.01),
        "w_gu": nrm(ks[20], (DEPTH, N_EXPERTS, D_MODEL, 2 * D_FF), D_MODEL ** -0.5),
        "b_gu": nrm(ks[21], (DEPTH, N_EXPERTS, 2 * D_FF), 0.02),
        "w_down": nrm(ks[22], (DEPTH, N_EXPERTS, D_FF, D_MODEL), DN_BETA * D_FF ** -0.5),
        "b_down": nrm(ks[23], (DEPTH, N_EXPERTS, D_MODEL), 0.02),
        "w_pg": nrm(ks[24], (DEPTH, D_MODEL, D_MODEL), D_MODEL ** -0.5),
        "w_ple": nrm(ks[25], (DEPTH, PLE_DIM, D_MODEL), DN_BETA * PLE_DIM ** -0.5),
        "ln2_g": 1.0 + nrm(ks[26], (DEPTH, D_MODEL), 0.02),
        "ln2_b": nrm(ks[27], (DEPTH, D_MODEL), 0.02),
    }


def reference(x_prompt, x_sample, p_prompt, p_sample, ln_emb_g, ln_emb_b, w_in, b_in, rpb,
              sgu_ln_g, sgu_ln_b, sgu_w, sgu_b, w_pa, w_pb, w_o, ln1_g, ln1_b, w_router, b_router,
              w_gu, b_gu, w_down, b_down, w_pg, w_ple, ln2_g, ln2_b):
    weights = (ln_emb_g, ln_emb_b, w_in, b_in, rpb, sgu_ln_g, sgu_ln_b, sgu_w, sgu_b,
               w_pa, w_pb, w_o, ln1_g, ln1_b, w_router, b_router, w_gu, b_gu, w_down, b_down,
               w_pg, w_ple, ln2_g, ln2_b)
    y_prompt = _encode(x_prompt, p_prompt, *weights)
    y_sample = _encode(x_sample, p_sample, *weights)
    return (y_prompt, y_sample)
```

```python
import functools

import jax
import jax.numpy as jnp
from jax import lax
from jax.experimental import pallas as pl
from jax.experimental.pallas import tpu as pltpu

_F32 = jnp.float32
_BF16 = jnp.bfloat16
_I32 = jnp.int32
_U32 = jnp.uint32

LN_EPS = 1e-5
GRID_W = 64
WIN_R = 8
WIN_C = 16
NA_HEAD_DIM = 128
TOP_K = 4
SWIGLU_ALPHA = 1.702
SWIGLU_LIMIT = 7.0
MASK_VALUE = -1e30

V7X_VMEM_BYTES = 64 * 1024 * 1024
V7X_LANES = 128
V7X_MXU_DIM = 256
MOE_CHUNK = 2 * V7X_MXU_DIM
MOE_TAIL = V7X_MXU_DIM


def _cparams(semantics, vmem_bytes):
    limit = min(int(vmem_bytes), V7X_VMEM_BYTES - 4 * 1024 * 1024)
    return pltpu.CompilerParams(dimension_semantics=semantics, vmem_limit_bytes=limit)


def _nbytes(shape, dtype):
    n = 1
    for s in shape:
        n *= s
    return n * jnp.dtype(dtype).itemsize


def _tile(dim, pref):
    t = min(dim, pref)
    while dim % t:
        t //= 2
    return t


def _layer_norm_rows(x, g, b):
    mu = jnp.mean(x, axis=-1, keepdims=True)
    xc = x - mu
    var = jnp.mean(xc * xc, axis=-1, keepdims=True)
    return xc * lax.rsqrt(var + LN_EPS) * g + b


def _gelu(x):
    return 0.5 * x * (1.0 + lax.erf(x * 0.7071067811865476))


def _sigmoid(x):
    return 1.0 / (1.0 + jnp.exp(-x))


def _ln_emb_body(xa_ref, xb_ref, g_ref, b_ref, of_ref, ob_ref, *, na):
    i = pl.program_id(0)

    def emit(x_ref):
        y = _layer_norm_rows(x_ref[...], g_ref[...], b_ref[...])
        of_ref[...] = y
        ob_ref[...] = y.astype(_BF16)

    @pl.when(i < na)
    def _():
        emit(xa_ref)

    @pl.when(i >= na)
    def _():
        emit(xb_ref)


def _ln_emb(xa, xb, g, b):
    d = xa.shape[1]
    tm = _tile(min(xa.shape[0], xb.shape[0]), 256)
    na, nb = xa.shape[0] // tm, xb.shape[0] // tm
    n = xa.shape[0] + xb.shape[0]
    blk = _nbytes((tm, d), _F32)
    return pl.pallas_call(
        functools.partial(_ln_emb_body, na=na),
        grid=(na + nb,),
        in_specs=[
            pl.BlockSpec((tm, d), lambda i: (jnp.minimum(i, na - 1), 0)),
            pl.BlockSpec((tm, d), lambda i: (jnp.maximum(i - na, 0), 0)),
            pl.BlockSpec((1, d), lambda i: (0, 0)),
            pl.BlockSpec((1, d), lambda i: (0, 0)),
        ],
        out_specs=[pl.BlockSpec((tm, d), lambda i: (i, 0)),
                   pl.BlockSpec((tm, d), lambda i: (i, 0))],
        out_shape=[jax.ShapeDtypeStruct((n, d), _F32), jax.ShapeDtypeStruct((n, d), _BF16)],
        compiler_params=_cparams(("arbitrary",), 12 * blk),
        name="ln_emb",
    )(xa, xb, g.reshape(1, d), b.reshape(1, d))


def _ln_mid_body(s_ref, g_ref, b_ref, hf_ref, hb_ref, hp_ref):
    y = _layer_norm_rows(s_ref[...], g_ref[...], b_ref[...])
    hf_ref[...] = y
    yb = y.astype(_BF16)
    hb_ref[...] = yb
    bits = lax.bitcast_convert_type(yb.astype(_F32), _U32)
    half = bits.shape[1] // 2
    hp_ref[...] = bits[:, :half] | (bits[:, half:] >> 16)


def _ln_mid(s, g, b):
    n, d = s.shape
    tm = _tile(n, 256)
    blk = _nbytes((tm, d), _F32)
    return pl.pallas_call(
        _ln_mid_body,
        grid=(n // tm,),
        in_specs=[pl.BlockSpec((tm, d), lambda i: (i, 0)),
                  pl.BlockSpec((1, d), lambda i: (0, 0)),
                  pl.BlockSpec((1, d), lambda i: (0, 0))],
        out_specs=[pl.BlockSpec((tm, d), lambda i: (i, 0)),
                   pl.BlockSpec((tm, d), lambda i: (i, 0)),
                   pl.BlockSpec((tm, d // 2), lambda i: (i, 0))],
        out_shape=[jax.ShapeDtypeStruct((n, d), _F32),
                   jax.ShapeDtypeStruct((n, d), _BF16),
                   jax.ShapeDtypeStruct((n, d // 2), _U32)],
        compiler_params=_cparams(("arbitrary",), 12 * blk),
        name="ln_mid",
    )(s, g.reshape(1, d), b.reshape(1, d))


def _ln_out_body(s_ref, g_ref, b_ref, o_ref):
    o_ref[...] = _layer_norm_rows(s_ref[...], g_ref[...], b_ref[...])


def _ln_out(s, g, b, row_off, rows):
    d = s.shape[1]
    tm = _tile(rows, 256)
    off = row_off // tm
    blk = _nbytes((tm, d), _F32)
    return pl.pallas_call(
        _ln_out_body,
        grid=(rows // tm,),
        in_specs=[pl.BlockSpec((tm, d), lambda i: (i + off, 0)),
                  pl.BlockSpec((1, d), lambda i: (0, 0)),
                  pl.BlockSpec((1, d), lambda i: (0, 0))],
        out_specs=pl.BlockSpec((tm, d), lambda i: (i, 0)),
        out_shape=jax.ShapeDtypeStruct((rows, d), _F32),
        compiler_params=_cparams(("arbitrary",), 8 * blk),
        name="ln_out",
    )(s, g.reshape(1, d), b.reshape(1, d))


def _in_proj_body(a_ref, w_ref, b_ref, o_ref):
    acc = jnp.dot(a_ref[...], w_ref[...], preferred_element_type=_F32)
    o_ref[...] = (acc + b_ref[...]).astype(o_ref.dtype)


def _in_proj(a, w, bias, layer, col_off, cols, out_dtype):
    m, k = a.shape
    bm = _tile(m, 2048)
    bn = _tile(cols, 512)
    joff = col_off // bn
    bias3 = bias.reshape(bias.shape[0], 1, bias.shape[1])
    vmem = (_nbytes((bm, k), a.dtype) + 2 * _nbytes((k, bn), w.dtype)
            + 2 * _nbytes((bm, bn), out_dtype) + 2 * _nbytes((bm, bn), _F32))
    return pl.pallas_call(
        _in_proj_body,
        grid=(m // bm, cols // bn),
        in_specs=[
            pl.BlockSpec((bm, k), lambda i, j: (i, 0), pipeline_mode=pl.Buffered(1)),
            pl.BlockSpec((None, k, bn), lambda i, j: (layer, 0, j + joff)),
            pl.BlockSpec((None, 1, bn), lambda i, j: (layer, 0, j + joff)),
        ],
        out_specs=pl.BlockSpec((bm, bn), lambda i, j: (i, j)),
        out_shape=jax.ShapeDtypeStruct((m, cols), out_dtype),
        compiler_params=_cparams(("arbitrary", "arbitrary"), vmem + (4 << 20)),
        name="in_proj",
    )(a, w, bias3)


def _merge_body(na_ref, sg_ref, wa_ref, wb_ref, ga_ref, gb_ref, o_ref):
    ya = jnp.dot(na_ref[...], wa_ref[...], preferred_element_type=_F32)
    yb = jnp.dot(sg_ref[...], wb_ref[...], preferred_element_type=_F32)
    merged = _sigmoid(ga_ref[...]) * ya + _sigmoid(gb_ref[...]) * yb
    o_ref[...] = merged.astype(o_ref.dtype)


def _merge(na_out, sgu_out, w_pa, w_pb, zrest, layer, ga_col, gb_col):
    m, ka = na_out.shape
    kb = sgu_out.shape[1]
    d = w_pa.shape[2]
    bm = _tile(m, 1024)
    bn = _tile(d, 512)
    ja, jb = ga_col // bn, gb_col // bn
    vmem = 2 * (_nbytes((bm, ka), _BF16) + _nbytes((bm, kb), _BF16) + _nbytes((ka, bn), _F32)
                + _nbytes((kb, bn), _F32) + 2 * _nbytes((bm, bn), _F32) + _nbytes((bm, bn), _BF16))
    vmem += 3 * _nbytes((bm, bn), _F32)
    return pl.pallas_call(
        _merge_body,
        grid=(m // bm, d // bn),
        in_specs=[
            pl.BlockSpec((bm, ka), lambda i, j: (i, 0)),
            pl.BlockSpec((bm, kb), lambda i, j: (i, 0)),
            pl.BlockSpec((None, ka, bn), lambda i, j: (layer, 0, j)),
            pl.BlockSpec((None, kb, bn), lambda i, j: (layer, 0, j)),
            pl.BlockSpec((bm, bn), lambda i, j: (i, j + ja)),
            pl.BlockSpec((bm, bn), lambda i, j: (i, j + jb)),
        ],
        out_specs=pl.BlockSpec((bm, bn), lambda i, j: (i, j)),
        out_shape=jax.ShapeDtypeStruct((m, d), _BF16),
        compiler_params=_cparams(("arbitrary", "arbitrary"), vmem + (4 << 20)),
        name="merge",
    )(na_out, sgu_out, w_pa, w_pb, zrest, zrest)


def _out_proj_body(a_ref, w_ref, x_ref, o_ref, *, alpha):
    acc = jnp.dot(a_ref[...], w_ref[...], preferred_element_type=_F32)
    o_ref[...] = alpha * x_ref[...] + acc


def _out_proj(a, w, resid, layer, alpha):
    m, k = a.shape
    d = w.shape[2]
    bm = _tile(m, 1024)
    bn = _tile(d, 512)
    vmem = 2 * (_nbytes((bm, k), a.dtype) + _nbytes((k, bn), w.dtype) + 2 * _nbytes((bm, bn), _F32))
    vmem += 2 * _nbytes((bm, bn), _F32)
    return pl.pallas_call(
        functools.partial(_out_proj_body, alpha=alpha),
        grid=(m // bm, d // bn),
        in_specs=[
            pl.BlockSpec((bm, k), lambda i, j: (i, 0)),
            pl.BlockSpec((None, k, bn), lambda i, j: (layer, 0, j)),
            pl.BlockSpec((bm, bn), lambda i, j: (i, j)),
        ],
        out_specs=pl.BlockSpec((bm, bn), lambda i, j: (i, j)),
        out_shape=jax.ShapeDtypeStruct((m, d), _F32),
        compiler_params=_cparams(("arbitrary", "arbitrary"), vmem + (4 << 20)),
        name="out_proj",
    )(a, w, resid)


def _ple_gate_body(a_ref, w_ref, p_ref, wp_ref, h_ref, f_ref, o_ref, *, alpha):
    gate = _sigmoid(jnp.dot(a_ref[...], w_ref[...], preferred_element_type=_F32))
    ple = jnp.dot(p_ref[...], wp_ref[...], preferred_element_type=_F32)
    o_ref[...] = alpha * h_ref[...] + f_ref[...] + gate * ple


def _ple_gate(hb, w_pg, p, w_ple, hf, f, layer, alpha):
    m, k = hb.shape
    d = w_pg.shape[2]
    kp = p.shape[1]
    bm = _tile(m, 512)
    bn = _tile(d, 512)
    vmem = 2 * (_nbytes((bm, k), hb.dtype) + _nbytes((k, bn), w_pg.dtype) + _nbytes((bm, kp), _F32)
                + _nbytes((kp, bn), _F32) + 3 * _nbytes((bm, bn), _F32))
    vmem += 3 * _nbytes((bm, bn), _F32)
    return pl.pallas_call(
        functools.partial(_ple_gate_body, alpha=alpha),
        grid=(d // bn, m // bm),
        in_specs=[
            pl.BlockSpec((bm, k), lambda j, i: (i, 0)),
            pl.BlockSpec((None, k, bn), lambda j, i: (layer, 0, j)),
            pl.BlockSpec((bm, kp), lambda j, i: (i, 0)),
            pl.BlockSpec((None, kp, bn), lambda j, i: (layer, 0, j)),
            pl.BlockSpec((bm, bn), lambda j, i: (i, j)),
            pl.BlockSpec((bm, bn), lambda j, i: (i, j)),
        ],
        out_specs=pl.BlockSpec((bm, bn), lambda j, i: (i, j)),
        out_shape=jax.ShapeDtypeStruct((m, d), _F32),
        compiler_params=_cparams(("arbitrary", "arbitrary"), vmem + (4 << 20)),
        name="ple_gate",
    )(hb, w_pg, p, w_ple, hf, f)


def _na_bias_table(rpb_l):
    heads = rpb_l.shape[0]
    cols = jnp.arange(GRID_W)
    col_start = jnp.clip(cols - WIN_C // 2, 0, GRID_W - WIN_C)
    kc = jnp.arange(GRID_W)
    in_win = (kc[None, :] >= col_start[:, None]) & (kc[None, :] < col_start[:, None] + WIN_C)
    dc = jnp.clip(kc[None, :] - cols[:, None] + (WIN_C - 1), 0, 2 * WIN_C - 2)
    d = jnp.arange(WIN_R)
    j = jnp.arange(WIN_R)
    dr = j[None, :] - d[:, None] + (WIN_R - 1)
    tab = rpb_l[:, dr[:, :, None, None], dc[None, None, :, :]]
    tab = jnp.where(in_win[None, None, None], tab.astype(_F32), MASK_VALUE)
    tab = jnp.transpose(tab, (0, 1, 3, 2, 4))
    return tab.reshape(heads, WIN_R, GRID_W, WIN_R * GRID_W)


def _na_body(q_ref, k_ref, v_ref, bias_ref, o_ref, *, rows, scale):
    n_keys = WIN_R * GRID_W

    def row_fn(r, carry):
        row_start = jnp.clip(r - WIN_R // 2, 0, rows - WIN_R)
        q0 = pl.multiple_of(r * GRID_W, GRID_W)
        k0 = pl.multiple_of(row_start * GRID_W, GRID_W)
        q = q_ref[pl.ds(q0, GRID_W), :]
        kw = k_ref[pl.ds(k0, n_keys), :]
        vw = v_ref[pl.ds(k0, n_keys), :]
        s = lax.dot_general(q, kw, (((1,), (1,)), ((), ())), preferred_element_type=_F32)
        s = s * scale + bias_ref[r - row_start]
        m = jnp.max(s, axis=-1, keepdims=True)
        e = jnp.exp(s - m)
        p = e / jnp.sum(e, axis=-1, keepdims=True)
        o = jnp.dot(p.astype(_BF16), vw, preferred_element_type=_F32)
        o_ref[pl.ds(q0, GRID_W), :] = o.astype(o_ref.dtype)
        return carry

    lax.fori_loop(0, rows, row_fn, 0)


def _neighbourhood_attention(zqkv, bias_tab, row_off, batch, seq, heads):
    assert seq % GRID_W == 0 and row_off % seq == 0
    rows = seq // GRID_W
    assert rows >= WIN_R
    boff = row_off // seq
    dh = NA_HEAD_DIM
    n_keys = WIN_R * GRID_W
    vmem = 2 * (4 * _nbytes((seq, dh), _BF16) + _nbytes((WIN_R, GRID_W, n_keys), _F32)) + (8 << 20)
    return pl.pallas_call(
        functools.partial(_na_body, rows=rows, scale=dh ** -0.5),
        grid=(heads, batch),
        in_specs=[
            pl.BlockSpec((seq, dh), lambda h, b: (b + boff, h)),
            pl.BlockSpec((seq, dh), lambda h, b: (b + boff, heads + h)),
            pl.BlockSpec((seq, dh), lambda h, b: (b + boff, 2 * heads + h)),
            pl.BlockSpec((None, WIN_R, GRID_W, n_keys), lambda h, b: (h, 0, 0, 0)),
        ],
        out_specs=pl.BlockSpec((seq, dh), lambda h, b: (b, h)),
        out_shape=jax.ShapeDtypeStruct((batch * seq, heads * dh), _BF16),
        compiler_params=_cparams(("arbitrary", "arbitrary"), vmem),
        name="neighbourhood_attention",
    )(zqkv, zqkv, zqkv, bias_tab)


def _sgu_body(u_ref, v_ref, g_ref, b_ref, ws_ref, bs_ref, o_ref, *, groups):
    u = _gelu(u_ref[...])
    vn = _layer_norm_rows(_gelu(v_ref[...]), g_ref[...], b_ref[...])
    gw = u.shape[1] // groups
    for g in range(groups):
        sl = slice(g * gw, (g + 1) * gw)
        s = jnp.dot(ws_ref[g], vn[:, sl], preferred_element_type=_F32) + bs_ref[g]
        o_ref[:, sl] = (u[:, sl] * s).astype(o_ref.dtype)


def _spatial_gating(zrest, ln_g, ln_b, w_s, b_s, layer, u_col, v_col, width):
    n = zrest.shape[0]
    groups, chunk = w_s.shape[1], w_s.shape[2]
    ju, jv = u_col // width, v_col // width
    blk = _nbytes((chunk, width), _F32)
    return pl.pallas_call(
        functools.partial(_sgu_body, groups=groups),
        grid=(n // chunk,),
        in_specs=[
            pl.BlockSpec((chunk, width), lambda i: (i, ju)),
            pl.BlockSpec((chunk, width), lambda i: (i, jv)),
            pl.BlockSpec((None, 1, width), lambda i: (layer, 0, 0)),
            pl.BlockSpec((None, 1, width), lambda i: (layer, 0, 0)),
            pl.BlockSpec((None, groups, chunk, chunk), lambda i: (layer, 0, 0, 0)),
            pl.BlockSpec((None, groups, chunk, 1), lambda i: (layer, 0, 0, 0)),
        ],
        out_specs=pl.BlockSpec((chunk, width), lambda i: (i, 0)),
        out_shape=jax.ShapeDtypeStruct((n, width), _BF16),
        compiler_params=_cparams(("arbitrary",), 12 * blk + (4 << 20)),
        name="spatial_gating",
    )(zrest, zrest, ln_g.reshape(-1, 1, width), ln_b.reshape(-1, 1, width), w_s,
      b_s.reshape(b_s.shape[0], groups, chunk, 1))


def _router_body(h_ref, w_ref, b_ref, idx_ref, gate_ref, rank_ref, cnt_ref, carry_ref):
    i = pl.program_id(0)

    @pl.when(i == 0)
    def _():
        carry_ref[...] = jnp.zeros_like(carry_ref)

    logits = jnp.dot(h_ref[...], w_ref[...], preferred_element_type=_F32,
                     precision=lax.Precision.HIGHEST) + b_ref[...]
    tm, n_exp = logits.shape
    e_iota = lax.broadcasted_iota(_I32, (tm, n_exp), 1)
    k_iota = lax.broadcasted_iota(_I32, (tm, TOP_K), 1)
    work = logits
    sels, vals = [], []
    idx_out = jnp.zeros((tm, TOP_K), _I32)
    for k in range(TOP_K):
        m = jnp.max(work, axis=-1, keepdims=True)
        idx = jnp.min(jnp.where(work == m, e_iota, n_exp), axis=-1, keepdims=True)
        sel = e_iota == idx
        sels.append(sel)
        vals.append(m)
        idx_out = jnp.where(k_iota == k, idx, idx_out)
        work = jnp.where(sel, -jnp.inf, work)
    exps = [jnp.exp(v - vals[0]) for v in vals]
    denom = exps[0]
    for e in exps[1:]:
        denom = denom + e
    gate_out = jnp.zeros((tm, TOP_K), _F32)
    for k in range(TOP_K):
        gate_out = jnp.where(k_iota == k, exps[k] / denom, gate_out)

    onehot = jnp.zeros((tm, n_exp), _F32)
    for sel in sels:
        onehot = onehot + sel.astype(_F32)
    r_iota = lax.broadcasted_iota(_I32, (tm, tm), 0)
    c_iota = lax.broadcasted_iota(_I32, (tm, tm), 1)
    lower = (c_iota < r_iota).astype(_BF16)
    pos = jnp.dot(lower, onehot.astype(_BF16), preferred_element_type=_F32) + carry_ref[...]
    rank_out = jnp.zeros((tm, TOP_K), _I32)
    for k in range(TOP_K):
        rk = jnp.sum(jnp.where(sels[k], pos, 0.0), axis=-1, keepdims=True).astype(_I32)
        rank_out = jnp.where(k_iota == k, rk, rank_out)
    carry = carry_ref[...] + jnp.sum(onehot, axis=0, keepdims=True)
    carry_ref[...] = carry

    idx_ref[...] = idx_out
    gate_ref[...] = gate_out
    rank_ref[...] = rank_out
    cnt_ref[...] = carry.astype(_I32)


def _router(hf, w_router, b_router, layer):
    n, d = hf.shape
    n_exp = w_router.shape[2]
    tm = _tile(n, 512)
    vmem = 2 * _nbytes((tm, d), _F32) + 2 * _nbytes((d, V7X_LANES), _F32) + 6 * _nbytes((tm, tm), _F32)
    return pl.pallas_call(
        _router_body,
        grid=(n // tm,),
        in_specs=[
            pl.BlockSpec((tm, d), lambda i: (i, 0)),
            pl.BlockSpec((None, d, n_exp), lambda i: (layer, 0, 0)),
            pl.BlockSpec((None, 1, n_exp), lambda i: (layer, 0, 0)),
        ],
        out_specs=[
            pl.BlockSpec((tm, TOP_K), lambda i: (i, 0)),
            pl.BlockSpec((tm, TOP_K), lambda i: (i, 0)),
            pl.BlockSpec((tm, TOP_K), lambda i: (i, 0)),
            pl.BlockSpec((1, n_exp), lambda i: (0, 0)),
        ],
        out_shape=[
            jax.ShapeDtypeStruct((n, TOP_K), _I32),
            jax.ShapeDtypeStruct((n, TOP_K), _F32),
            jax.ShapeDtypeStruct((n, TOP_K), _I32),
            jax.ShapeDtypeStruct((1, n_exp), _I32),
        ],
        scratch_shapes=[pltpu.VMEM((1, n_exp), _F32)],
        compiler_params=_cparams(("arbitrary",), vmem + (8 << 20)),
        name="router",
    )(hf, w_router, b_router.reshape(b_router.shape[0], 1, n_exp))


def _dispatch_body(dest_ref, hp_ref, xs_in_ref, xs_ref, sem):
    del xs_in_ref
    tm = hp_ref.shape[0]

    def row_copy(t, k):
        dst = xs_ref.at[pl.ds(dest_ref[t * TOP_K + k], 1)]
        return pltpu.make_async_copy(hp_ref.at[pl.ds(t, 1)], dst, sem)

    def start(t, c):
        for k in range(TOP_K):
            row_copy(t, k).start()
        return c

    def wait(t, c):
        for k in range(TOP_K):
            row_copy(t, k).wait()
        return c

    lax.fori_loop(0, tm, start, 0)
    lax.fori_loop(0, tm, wait, 0)


def _dispatch(hp, dest_flat, n_slots):
    n, w = hp.shape
    tm = _tile(n, 256)
    xs0 = jnp.zeros((n_slots, w), hp.dtype)
    return pl.pallas_call(
        _dispatch_body,
        grid=(n // tm,),
        in_specs=[
            pl.BlockSpec((tm * TOP_K,), lambda i: (i,), memory_space=pltpu.SMEM),
            pl.BlockSpec((tm, w), lambda i: (i, 0)),
            pl.BlockSpec(memory_space=pl.ANY),
        ],
        out_specs=pl.BlockSpec(memory_space=pl.ANY),
        out_shape=jax.ShapeDtypeStruct((n_slots, w), hp.dtype),
        scratch_shapes=[pltpu.SemaphoreType.DMA(())],
        input_output_aliases={2: 0},
        compiler_params=_cparams(("arbitrary",), 4 * _nbytes((tm, w), hp.dtype) + (4 << 20)),
        name="moe_dispatch",
    )(dest_flat, hp, xs0)


def _for_each_chunk(rows, fn):
    rows_p = ((rows + (MOE_TAIL - 1)) // MOE_TAIL) * MOE_TAIL
    n_full = rows_p // MOE_CHUNK

    def full(c, carry):
        fn(pl.multiple_of(c * MOE_CHUNK, MOE_CHUNK), MOE_CHUNK)
        return carry

    lax.fori_loop(0, n_full, full, 0)

    @pl.when(rows_p % MOE_CHUNK != 0)
    def _():
        fn(pl.multiple_of(n_full * MOE_CHUNK, MOE_TAIL), MOE_TAIL)


def _unpack_halves(words):
    lo_cols = lax.bitcast_convert_type(words & jnp.uint32(0xFFFF0000), _F32)
    hi_cols = lax.bitcast_convert_type(words << 16, _F32)
    return lo_cols, hi_cols


def _gmm_gate_up_body(sbe_ref, blk_ref, rows_ref, nlive_ref, x_ref, wg_ref, wu_ref, bg_ref, bu_ref, o_ref):
    del sbe_ref, blk_ref, nlive_ref
    s = pl.program_id(0)
    half = x_ref.shape[1]

    def chunk(r0, size):
        xa, xb = _unpack_halves(x_ref[pl.ds(r0, size), :])
        g = (jnp.dot(xa, wg_ref[:half, :], preferred_element_type=_F32)
             + jnp.dot(xb, wg_ref[half:, :], preferred_element_type=_F32) + bg_ref[...])
        u = (jnp.dot(xa, wu_ref[:half, :], preferred_element_type=_F32)
             + jnp.dot(xb, wu_ref[half:, :], preferred_element_type=_F32) + bu_ref[...])
        g = jnp.minimum(g, SWIGLU_LIMIT)
        u = jnp.clip(u, -SWIGLU_LIMIT, SWIGLU_LIMIT)
        act = (u + 1.0) * (g * _sigmoid(g * SWIGLU_ALPHA))
        o_ref[pl.ds(r0, size), :] = act.astype(o_ref.dtype)

    _for_each_chunk(rows_ref[s], chunk)


def _gmm_down_body(sbe_ref, blk_ref, rows_ref, nlive_ref, a_ref, w_ref, b_ref, o_ref):
    del sbe_ref, blk_ref, nlive_ref
    s = pl.program_id(0)

    def chunk(r0, size):
        y = jnp.dot(a_ref[pl.ds(r0, size), :], w_ref[...], preferred_element_type=_F32) + b_ref[...]
        o_ref[pl.ds(r0, size), :] = y

    _for_each_chunk(rows_ref[s], chunk)


def _expert_mlp(xs, meta, w_gu, b_gu, w_down, b_down, layer, cap):
    sbe, blk, sb_rows, n_live = meta
    n_sb = sbe.shape[0]
    n_slots, half = xs.shape
    d = 2 * half
    n_exp = w_gu.shape[1]
    d_ff = w_gu.shape[3] // 2
    tf = _tile(d_ff, 256)
    nf = d_ff // tf
    b_gu4 = b_gu.reshape(b_gu.shape[0], n_exp, 1, 2 * d_ff)

    def col(s, j, nl, n_tiles):
        return jnp.where(s < nl[0], j, n_tiles - 1)

    vmem1 = (_nbytes((cap, half), _U32) + 4 * _nbytes((d, tf), _F32) + 2 * _nbytes((cap, tf), _BF16)
             + 2 * _nbytes((MOE_CHUNK, d), _F32) + 6 * _nbytes((MOE_CHUNK, tf), _F32))
    act = pl.pallas_call(
        _gmm_gate_up_body,
        grid_spec=pltpu.PrefetchScalarGridSpec(
            num_scalar_prefetch=4,
            grid=(n_sb, nf),
            in_specs=[
                pl.BlockSpec((cap, half), lambda s, j, e, b, r, nl: (b[s], 0),
                             pipeline_mode=pl.Buffered(1)),
                pl.BlockSpec((None, None, d, tf), lambda s, j, e, b, r, nl: (layer, e[s], 0, col(s, j, nl, nf))),
                pl.BlockSpec((None, None, d, tf), lambda s, j, e, b, r, nl: (layer, e[s], 0, nf + col(s, j, nl, nf))),
                pl.BlockSpec((None, None, 1, tf), lambda s, j, e, b, r, nl: (layer, e[s], 0, col(s, j, nl, nf))),
                pl.BlockSpec((None, None, 1, tf), lambda s, j, e, b, r, nl: (layer, e[s], 0, nf + col(s, j, nl, nf))),
            ],
            out_specs=pl.BlockSpec((cap, tf), lambda s, j, e, b, r, nl: (b[s], col(s, j, nl, nf))),
        ),
        out_shape=jax.ShapeDtypeStruct((n_slots, d_ff), _BF16),
        compiler_params=_cparams(("arbitrary", "arbitrary"), vmem1 + (4 << 20)),
        name="moe_gate_up",
    )(sbe, blk, sb_rows, n_live, xs, w_gu, w_gu, b_gu4, b_gu4)

    tn = _tile(d, 256)
    nn = d // tn
    vmem2 = (_nbytes((cap, d_ff), _BF16) + 2 * _nbytes((d_ff, tn), _F32) + 2 * _nbytes((cap, tn), _F32)
             + 3 * _nbytes((MOE_CHUNK, tn), _F32))
    ys = pl.pallas_call(
        _gmm_down_body,
        grid_spec=pltpu.PrefetchScalarGridSpec(
            num_scalar_prefetch=4,
            grid=(n_sb, nn),
            in_specs=[
                pl.BlockSpec((cap, d_ff), lambda s, j, e, b, r, nl: (b[s], 0),
                             pipeline_mode=pl.Buffered(1)),
                pl.BlockSpec((None, None, d_ff, tn), lambda s, j, e, b, r, nl: (layer, e[s], 0, col(s, j, nl, nn))),
                pl.BlockSpec((None, None, 1, tn), lambda s, j, e, b, r, nl: (layer, e[s], 0, col(s, j, nl, nn))),
            ],
            out_specs=pl.BlockSpec((cap, tn), lambda s, j, e, b, r, nl: (b[s], col(s, j, nl, nn))),
        ),
        out_shape=jax.ShapeDtypeStruct((n_slots, d), _F32),
        compiler_params=_cparams(("arbitrary", "arbitrary"), vmem2 + (4 << 20)),
        name="moe_down",
    )(sbe, blk, sb_rows, n_live, act, w_down, b_down.reshape(b_down.shape[0], n_exp, 1, d))
    return ys


def _combine_body(dest_ref, gate_ref, ys_ref, o_ref, buf, sem):
    tm = gate_ref.shape[0]

    def row_copy(t, k):
        src = ys_ref.at[pl.ds(dest_ref[t * TOP_K + k], 1)]
        return pltpu.make_async_copy(src, buf.at[k, pl.ds(t, 1)], sem)

    def start(t, c):
        for k in range(TOP_K):
            row_copy(t, k).start()
        return c

    def wait(t, c):
        for k in range(TOP_K):
            row_copy(t, k).wait()
        return c

    lax.fori_loop(0, tm, start, 0)
    lax.fori_loop(0, tm, wait, 0)
    gate = gate_ref[...]
    acc = gate[:, 0:1] * buf[0]
    for k in range(1, TOP_K):
        acc = acc + gate[:, k:k + 1] * buf[k]
    o_ref[...] = acc


def _combine(ys, dest_flat, gate):
    n = gate.shape[0]
    d = ys.shape[1]
    tm = _tile(n, 256)
    return pl.pallas_call(
        _combine_body,
        grid=(n // tm,),
        in_specs=[
            pl.BlockSpec((tm * TOP_K,), lambda i: (i,), memory_space=pltpu.SMEM),
            pl.BlockSpec((tm, TOP_K), lambda i: (i, 0)),
            pl.BlockSpec(memory_space=pl.ANY),
        ],
        out_specs=pl.BlockSpec((tm, d), lambda i: (i, 0)),
        out_shape=jax.ShapeDtypeStruct((n, d), _F32),
        scratch_shapes=[pltpu.VMEM((TOP_K, tm, d), _F32), pltpu.SemaphoreType.DMA(())],
        compiler_params=_cparams(("arbitrary",), (TOP_K + 4) * _nbytes((tm, d), _F32) + (4 << 20)),
        name="moe_combine",
    )(dest_flat, gate, ys)


def _moe_capacity(n_assign, n_exp):
    mean = -(-n_assign // n_exp)
    return -(-(mean + mean // 8) // MOE_TAIL) * MOE_TAIL


def _moe_plan(top_idx, rank, counts, cap, n_sb):
    n_exp = counts.shape[0]
    sb_per_e = (counts + cap - 1) // cap
    sb_end = jnp.cumsum(sb_per_e)
    sb_start = sb_end - sb_per_e
    n_live = sb_end[-1]
    sidx = jnp.arange(n_sb, dtype=_I32)
    e_of = jnp.minimum(jnp.sum(sidx[:, None] >= sb_end[None, :], axis=1), n_exp - 1).astype(_I32)
    live = sidx < n_live
    local = sidx - sb_start[e_of]
    rows = jnp.where(live, jnp.clip(counts[e_of] - local * cap, 0, cap), 0).astype(_I32)
    last = jnp.maximum(n_live - 1, 0)
    sbe = jnp.where(live, e_of, e_of[last]).astype(_I32)
    blk = jnp.where(live, sidx, last).astype(_I32)
    dest = (sb_start * cap)[top_idx] + rank
    return (sbe, blk, rows, n_live.reshape(1).astype(_I32)), dest.reshape(-1).astype(_I32)


def _moe(hf, hp, w_router, b_router, w_gu, b_gu, w_down, b_down, layer):
    n = hf.shape[0]
    n_exp = w_router.shape[2]
    top_idx, gate, rank, counts = _router(hf, w_router, b_router, layer)
    n_assign = n * TOP_K
    cap = _moe_capacity(n_assign, n_exp)
    n_sb = n_assign // cap + n_exp
    meta, dest = _moe_plan(top_idx, rank, counts[0], cap, n_sb)
    xs = _dispatch(hp, dest, n_sb * cap)
    ys = _expert_mlp(xs, meta, w_gu, b_gu, w_down, b_down, layer, cap)
    return _combine(ys, dest, gate)


def kernel(x_prompt, x_sample, p_prompt, p_sample, ln_emb_g, ln_emb_b, w_in, b_in, rpb, sgu_ln_g, sgu_ln_b, sgu_w, sgu_b, w_pa, w_pb, w_o, ln1_g, ln1_b, w_router, b_router, w_gu, b_gu, w_down, b_down, w_pg, w_ple, ln2_g, ln2_b):
    depth = w_in.shape[0]
    d = x_prompt.shape[-1]
    ba, ta = x_prompt.shape[:2]
    bb, tb = x_sample.shape[:2]
    na_rows, nb_rows = ba * ta, bb * tb
    na_width = w_pa.shape[1]
    sgu_width = w_pb.shape[1]
    heads = na_width // NA_HEAD_DIM
    alpha = (2.0 * depth) ** 0.25
    qkv_cols = 3 * na_width

    xf, xb = _ln_emb(x_prompt.reshape(na_rows, d), x_sample.reshape(nb_rows, d), ln_emb_g, ln_emb_b)
    p_all = jnp.concatenate([p_prompt.reshape(depth, na_rows, -1), p_sample.reshape(depth, nb_rows, -1)], axis=1)

    s2 = None
    for l in range(depth):
        if l > 0:
            xf, xb, _ = _ln_mid(s2, ln2_g[l - 1], ln2_b[l - 1])
        zqkv = _in_proj(xb, w_in, b_in, l, 0, qkv_cols, _BF16)
        zrest = _in_proj(xb, w_in, b_in, l, qkv_cols, w_in.shape[2] - qkv_cols, _F32)
        bias_tab = _na_bias_table(rpb[l])
        na_out = jnp.concatenate([
            _neighbourhood_attention(zqkv, bias_tab, 0, ba, ta, heads),
            _neighbourhood_attention(zqkv, bias_tab, na_rows, bb, tb, heads)], axis=0)
        sgu_out = _spatial_gating(zrest, sgu_ln_g, sgu_ln_b, sgu_w, sgu_b, l, 0, sgu_width, sgu_width)
        merged = _merge(na_out, sgu_out, w_pa, w_pb, zrest, l, 2 * sgu_width, 2 * sgu_width + d)
        s1 = _out_proj(merged, w_o, xf, l, alpha)
        hf, hb, hp = _ln_mid(s1, ln1_g[l], ln1_b[l])
        f = _moe(hf, hp, w_router, b_router, w_gu, b_gu, w_down, b_down, l)
        s2 = _ple_gate(hb, w_pg, p_all[l], w_ple, hf, f, l, alpha)

    y_a = _ln_out(s2, ln2_g[depth - 1], ln2_b[depth - 1], 0, na_rows)
    y_b = _ln_out(s2, ln2_g[depth - 1], ln2_b[depth - 1], na_rows, nb_rows)
    return (y_a.reshape(ba, ta, d), y_b.reshape(bb, tb, d))
```

```python
import functools

import jax
import jax.numpy as jnp
import numpy as np
from jax import lax
from jax.experimental import pallas as pl
from jax.experimental.pallas import tpu as pltpu

_F32 = jnp.float32
_BF16 = jnp.bfloat16
_I32 = jnp.int32
_U32 = jnp.uint32

LN_EPS = 1e-5
GRID_W = 64
WIN_R = 8
WIN_C = 16
NA_HEAD_DIM = 128
TOP_K = 4
SWIGLU_ALPHA = 1.702
SWIGLU_LIMIT = 7.0
MASK_VALUE = -1e30
NA_BLOCK_ROWS = 4
NA_KEY_ROWS = 12
NA_UNROLL = 4

V7X_VMEM_BYTES = 64 * 1024 * 1024
V7X_LANES = 128
V7X_SUBLANES = 8
V7X_MXU_DIM = 256
MOE_CHUNK_GATE_UP = 2 * V7X_MXU_DIM
MOE_CHUNK_DOWN = 4 * V7X_MXU_DIM
MOE_TAIL = V7X_MXU_DIM // 2
MOE_ZERO_ROWS = MOE_TAIL + V7X_SUBLANES


def _cparams(semantics, vmem_bytes):
    limit = min(int(vmem_bytes), V7X_VMEM_BYTES - 4 * 1024 * 1024)
    return pltpu.CompilerParams(dimension_semantics=semantics, vmem_limit_bytes=limit)


def _nbytes(shape, dtype):
    n = 1
    for s in shape:
        n *= s
    return n * jnp.dtype(dtype).itemsize


def _tile(dim, pref):
    t = min(dim, pref)
    while dim % t:
        t //= 2
    return t


def _layer_norm_rows(x, g, b):
    mu = jnp.mean(x, axis=-1, keepdims=True)
    xc = x - mu
    var = jnp.mean(xc * xc, axis=-1, keepdims=True)
    return xc * lax.rsqrt(var + LN_EPS) * g + b


def _gelu(x):
    return 0.5 * x * (1.0 + lax.erf(x * 0.7071067811865476))


def _sigmoid(x):
    return 1.0 / (1.0 + jnp.exp(-x))


def _ln_emb_body(xa_ref, xb_ref, g_ref, b_ref, of_ref, ob_ref, *, na):
    i = pl.program_id(0)

    def emit(x_ref):
        y = _layer_norm_rows(x_ref[...], g_ref[...], b_ref[...])
        of_ref[...] = y
        ob_ref[...] = y.astype(_BF16)

    @pl.when(i < na)
    def _():
        emit(xa_ref)

    @pl.when(i >= na)
    def _():
        emit(xb_ref)


def _ln_emb(xa, xb, g, b):
    d = xa.shape[1]
    tm = _tile(min(xa.shape[0], xb.shape[0]), 256)
    na, nb = xa.shape[0] // tm, xb.shape[0] // tm
    n = xa.shape[0] + xb.shape[0]
    blk = _nbytes((tm, d), _F32)
    return pl.pallas_call(
        functools.partial(_ln_emb_body, na=na),
        grid=(na + nb,),
        in_specs=[
            pl.BlockSpec((tm, d), lambda i: (jnp.minimum(i, na - 1), 0)),
            pl.BlockSpec((tm, d), lambda i: (jnp.maximum(i - na, 0), 0)),
            pl.BlockSpec((1, d), lambda i: (0, 0)),
            pl.BlockSpec((1, d), lambda i: (0, 0)),
        ],
        out_specs=[pl.BlockSpec((tm, d), lambda i: (i, 0)),
                   pl.BlockSpec((tm, d), lambda i: (i, 0))],
        out_shape=[jax.ShapeDtypeStruct((n, d), _F32), jax.ShapeDtypeStruct((n, d), _BF16)],
        compiler_params=_cparams(("arbitrary",), 12 * blk),
        name="ln_emb",
    )(xa, xb, g.reshape(1, d), b.reshape(1, d))


def _ln_mid_body(s_ref, g_ref, b_ref, hf_ref, hb_ref, hp_ref):
    y = _layer_norm_rows(s_ref[...], g_ref[...], b_ref[...])
    hf_ref[...] = y
    yb = y.astype(_BF16)
    hb_ref[...] = yb
    bits = lax.bitcast_convert_type(yb.astype(_F32), _U32)
    half = bits.shape[1] // 2
    hp_ref[...] = bits[:, :half] | (bits[:, half:] >> 16)


def _ln_mid(s, g, b):
    n, d = s.shape
    tm = _tile(n, 256)
    blk = _nbytes((tm, d), _F32)
    return pl.pallas_call(
        _ln_mid_body,
        grid=(n // tm,),
        in_specs=[pl.BlockSpec((tm, d), lambda i: (i, 0)),
                  pl.BlockSpec((1, d), lambda i: (0, 0)),
                  pl.BlockSpec((1, d), lambda i: (0, 0))],
        out_specs=[pl.BlockSpec((tm, d), lambda i: (i, 0)),
                   pl.BlockSpec((tm, d), lambda i: (i, 0)),
                   pl.BlockSpec((tm, d // 2), lambda i: (i, 0))],
        out_shape=[jax.ShapeDtypeStruct((n, d), _F32),
                   jax.ShapeDtypeStruct((n, d), _BF16),
                   jax.ShapeDtypeStruct((n, d // 2), _U32)],
        compiler_params=_cparams(("arbitrary",), 12 * blk),
        name="ln_mid",
    )(s, g.reshape(1, d), b.reshape(1, d))


def _in_proj_body(a_ref, w_ref, b_ref, o_ref):
    acc = jnp.dot(a_ref[...], w_ref[...], preferred_element_type=_F32)
    o_ref[...] = (acc + b_ref[...]).astype(o_ref.dtype)


def _in_proj(a, w, bias, layer, col_off, cols, out_dtype):
    m, k = a.shape
    bm = _tile(m, 2048)
    bn = _tile(cols, 512)
    joff = col_off // bn
    bias3 = bias.reshape(bias.shape[0], 1, bias.shape[1])
    vmem = (_nbytes((bm, k), a.dtype) + 2 * _nbytes((k, bn), w.dtype)
            + 2 * _nbytes((bm, bn), out_dtype) + 2 * _nbytes((bm, bn), _F32))
    return pl.pallas_call(
        _in_proj_body,
        grid=(m // bm, cols // bn),
        in_specs=[
            pl.BlockSpec((bm, k), lambda i, j: (i, 0), pipeline_mode=pl.Buffered(1)),
            pl.BlockSpec((None, k, bn), lambda i, j: (layer, 0, j + joff)),
            pl.BlockSpec((None, 1, bn), lambda i, j: (layer, 0, j + joff)),
        ],
        out_specs=pl.BlockSpec((bm, bn), lambda i, j: (i, j)),
        out_shape=jax.ShapeDtypeStruct((m, cols), out_dtype),
        compiler_params=_cparams(("arbitrary", "arbitrary"), vmem + (4 << 20)),
        name="in_proj",
    )(a, w, bias3)


def _merge_body(na_ref, sg_ref, wa_ref, wb_ref, ga_ref, gb_ref, o_ref):
    ya = jnp.dot(na_ref[...], wa_ref[...], preferred_element_type=_F32)
    yb = jnp.dot(sg_ref[...], wb_ref[...], preferred_element_type=_F32)
    merged = _sigmoid(ga_ref[...]) * ya + _sigmoid(gb_ref[...]) * yb
    o_ref[...] = merged.astype(o_ref.dtype)


def _merge(na_out, sgu_out, w_pa, w_pb, zrest, layer, ga_col, gb_col):
    m, ka = na_out.shape
    kb = sgu_out.shape[1]
    d = w_pa.shape[2]
    bm = _tile(m, 1024)
    bn = _tile(d, 512)
    ja, jb = ga_col // bn, gb_col // bn
    vmem = 2 * (_nbytes((bm, ka), _BF16) + _nbytes((bm, kb), _BF16) + _nbytes((ka, bn), _F32)
                + _nbytes((kb, bn), _F32) + 2 * _nbytes((bm, bn), _F32) + _nbytes((bm, bn), _BF16))
    vmem += 3 * _nbytes((bm, bn), _F32)
    return pl.pallas_call(
        _merge_body,
        grid=(m // bm, d // bn),
        in_specs=[
            pl.BlockSpec((bm, ka), lambda i, j: (i, 0)),
            pl.BlockSpec((bm, kb), lambda i, j: (i, 0)),
            pl.BlockSpec((None, ka, bn), lambda i, j: (layer, 0, j)),
            pl.BlockSpec((None, kb, bn), lambda i, j: (layer, 0, j)),
            pl.BlockSpec((bm, bn), lambda i, j: (i, j + ja)),
            pl.BlockSpec((bm, bn), lambda i, j: (i, j + jb)),
        ],
        out_specs=pl.BlockSpec((bm, bn), lambda i, j: (i, j)),
        out_shape=jax.ShapeDtypeStruct((m, d), _BF16),
        compiler_params=_cparams(("arbitrary", "arbitrary"), vmem + (4 << 20)),
        name="merge",
    )(na_out, sgu_out, w_pa, w_pb, zrest, zrest)


def _out_proj_body(a_ref, w_ref, x_ref, o_ref, *, alpha):
    acc = jnp.dot(a_ref[...], w_ref[...], preferred_element_type=_F32)
    o_ref[...] = alpha * x_ref[...] + acc


def _out_proj(a, w, resid, layer, alpha):
    m, k = a.shape
    d = w.shape[2]
    bm = _tile(m, 1024)
    bn = _tile(d, 512)
    vmem = 2 * (_nbytes((bm, k), a.dtype) + _nbytes((k, bn), w.dtype) + 2 * _nbytes((bm, bn), _F32))
    vmem += 2 * _nbytes((bm, bn), _F32)
    return pl.pallas_call(
        functools.partial(_out_proj_body, alpha=alpha),
        grid=(m // bm, d // bn),
        in_specs=[
            pl.BlockSpec((bm, k), lambda i, j: (i, 0)),
            pl.BlockSpec((None, k, bn), lambda i, j: (layer, 0, j)),
            pl.BlockSpec((bm, bn), lambda i, j: (i, j)),
        ],
        out_specs=pl.BlockSpec((bm, bn), lambda i, j: (i, j)),
        out_shape=jax.ShapeDtypeStruct((m, d), _F32),
        compiler_params=_cparams(("arbitrary", "arbitrary"), vmem + (4 << 20)),
        name="out_proj",
    )(a, w, resid)


def _ple_gate_body(a_ref, w_ref, p_ref, wp_ref, o_ref):
    gate = _sigmoid(jnp.dot(a_ref[...], w_ref[...], preferred_element_type=_F32))
    ple = jnp.dot(p_ref[...], wp_ref[...], preferred_element_type=_F32)
    o_ref[...] = gate * ple


def _ple_gate(hb, w_pg, p, w_ple, layer):
    m, k = hb.shape
    d = w_pg.shape[2]
    kp = p.shape[1]
    bm = _tile(m, 2048)
    bn = _tile(d, 512)
    vmem = (_nbytes((bm, k), hb.dtype) + _nbytes((bm, kp), _F32)
            + 2 * (_nbytes((k, bn), w_pg.dtype) + _nbytes((kp, bn), _F32) + _nbytes((bm, bn), _F32))
            + 3 * _nbytes((bm, bn), _F32))
    return pl.pallas_call(
        _ple_gate_body,
        grid=(m // bm, d // bn),
        in_specs=[
            pl.BlockSpec((bm, k), lambda i, j: (i, 0), pipeline_mode=pl.Buffered(1)),
            pl.BlockSpec((None, k, bn), lambda i, j: (layer, 0, j)),
            pl.BlockSpec((bm, kp), lambda i, j: (i, 0), pipeline_mode=pl.Buffered(1)),
            pl.BlockSpec((None, kp, bn), lambda i, j: (layer, 0, j)),
        ],
        out_specs=pl.BlockSpec((bm, bn), lambda i, j: (i, j)),
        out_shape=jax.ShapeDtypeStruct((m, d), _F32),
        compiler_params=_cparams(("arbitrary", "arbitrary"), vmem + (4 << 20)),
        name="ple_gate",
    )(hb, w_pg, p, w_ple)


def _na_bias_table(rpb_l):
    heads = rpb_l.shape[0]
    half = WIN_R // 2
    i = np.arange(NA_BLOCK_ROWS)
    win_off = np.stack([np.zeros_like(i), i, np.full_like(i, NA_KEY_ROWS - WIN_R)])
    win_d = np.stack([i, np.full_like(i, half), half + i])
    kr = np.arange(NA_KEY_ROWS)
    j = kr[None, None, :] - win_off[:, :, None]
    row_ok = (j >= 0) & (j < WIN_R)
    dr = j - win_d[:, :, None] + (WIN_R - 1)
    row_sel = row_ok[..., None] & (dr[..., None] == np.arange(2 * WIN_R - 1))
    cols = np.arange(GRID_W)
    col_start = np.clip(cols - WIN_C // 2, 0, GRID_W - WIN_C)
    col_ok = (cols[None, :] >= col_start[:, None]) & (cols[None, :] < col_start[:, None] + WIN_C)
    dc = cols[None, :] - cols[:, None] + (WIN_C - 1)
    col_sel = col_ok[..., None] & (dc[..., None] == np.arange(2 * WIN_C - 1))
    tab = jnp.einsum("hab,pika,qcb->hpiqkc", rpb_l.astype(_F32), row_sel.astype(np.float32),
                     col_sel.astype(np.float32), precision=lax.Precision.HIGHEST)
    keep = row_ok[:, :, None, :, None] & col_ok[None, None, :, None, :]
    tab = jnp.where(keep[None], tab, MASK_VALUE)
    return tab.reshape(heads, 3, NA_BLOCK_ROWS * GRID_W, NA_KEY_ROWS * GRID_W)


def _na_body(q_ref, k_ref, v_ref, bias_ref, o_ref, *, rows, scale):
    n_q = NA_BLOCK_ROWS * GRID_W
    n_k = NA_KEY_ROWS * GRID_W
    n_blocks = rows // NA_BLOCK_ROWS

    def block_fn(ib, carry):
        r0 = ib * NA_BLOCK_ROWS
        kb0 = jnp.clip(r0 - WIN_R // 2, 0, rows - NA_KEY_ROWS)
        pattern = jnp.where(ib == 0, 0, jnp.where(ib == n_blocks - 1, 2, 1))
        q0 = pl.multiple_of(r0 * GRID_W, n_q)
        k0 = pl.multiple_of(kb0 * GRID_W, NA_BLOCK_ROWS * GRID_W)
        q = q_ref[pl.ds(q0, n_q), :]
        kw = k_ref[pl.ds(k0, n_k), :]
        vw = v_ref[pl.ds(k0, n_k), :]
        s = lax.dot_general(q, kw, (((1,), (1,)), ((), ())), preferred_element_type=_F32)
        s = s * scale + bias_ref[pattern]
        m = jnp.max(s, axis=-1, keepdims=True)
        e = jnp.exp(s - m)
        p = e / jnp.sum(e, axis=-1, keepdims=True)
        o = jnp.dot(p.astype(_BF16), vw, preferred_element_type=_F32)
        o_ref[pl.ds(q0, n_q), :] = o.astype(o_ref.dtype)
        return carry

    lax.fori_loop(0, n_blocks, block_fn, 0, unroll=NA_UNROLL)


def _neighbourhood_attention(zqkv, bias_tab, row_off, batch, seq, heads):
    assert seq % GRID_W == 0 and row_off % seq == 0
    rows = seq // GRID_W
    assert rows % NA_BLOCK_ROWS == 0 and rows >= NA_KEY_ROWS
    boff = row_off // seq
    dh = NA_HEAD_DIM
    n_q = NA_BLOCK_ROWS * GRID_W
    n_k = NA_KEY_ROWS * GRID_W
    vmem = 2 * (4 * _nbytes((seq, dh), _BF16) + _nbytes((3, n_q, n_k), _F32)) + 6 * _nbytes((n_q, n_k), _F32)
    return pl.pallas_call(
        functools.partial(_na_body, rows=rows, scale=dh ** -0.5),
        grid=(heads, batch),
        in_specs=[
            pl.BlockSpec((seq, dh), lambda h, b: (b + boff, h)),
            pl.BlockSpec((seq, dh), lambda h, b: (b + boff, heads + h)),
            pl.BlockSpec((seq, dh), lambda h, b: (b + boff, 2 * heads + h)),
            pl.BlockSpec((None, 3, n_q, n_k), lambda h, b: (h, 0, 0, 0)),
        ],
        out_specs=pl.BlockSpec((seq, dh), lambda h, b: (b, h)),
        out_shape=jax.ShapeDtypeStruct((batch * seq, heads * dh), _BF16),
        compiler_params=_cparams(("arbitrary", "arbitrary"), vmem),
        name="neighbourhood_attention",
    )(zqkv, zqkv, zqkv, bias_tab)


def _sgu_body(u_ref, v_ref, g_ref, b_ref, ws_ref, bs_ref, o_ref, *, groups):
    u = _gelu(u_ref[...])
    vn = _layer_norm_rows(_gelu(v_ref[...]), g_ref[...], b_ref[...])
    gw = u.shape[1] // groups
    for g in range(groups):
        sl = slice(g * gw, (g + 1) * gw)
        s = jnp.dot(ws_ref[g], vn[:, sl], preferred_element_type=_F32) + bs_ref[g]
        o_ref[:, sl] = (u[:, sl] * s).astype(o_ref.dtype)


def _spatial_gating(zrest, ln_g, ln_b, w_s, b_s, layer, u_col, v_col, width):
    n = zrest.shape[0]
    groups, chunk = w_s.shape[1], w_s.shape[2]
    ju, jv = u_col // width, v_col // width
    blk = _nbytes((chunk, width), _F32)
    return pl.pallas_call(
        functools.partial(_sgu_body, groups=groups),
        grid=(n // chunk,),
        in_specs=[
            pl.BlockSpec((chunk, width), lambda i: (i, ju)),
            pl.BlockSpec((chunk, width), lambda i: (i, jv)),
            pl.BlockSpec((None, 1, width), lambda i: (layer, 0, 0)),
            pl.BlockSpec((None, 1, width), lambda i: (layer, 0, 0)),
            pl.BlockSpec((None, groups, chunk, chunk), lambda i: (layer, 0, 0, 0)),
            pl.BlockSpec((None, groups, chunk, 1), lambda i: (layer, 0, 0, 0)),
        ],
        out_specs=pl.BlockSpec((chunk, width), lambda i: (i, 0)),
        out_shape=jax.ShapeDtypeStruct((n, width), _BF16),
        compiler_params=_cparams(("arbitrary",), 12 * blk + (4 << 20)),
        name="spatial_gating",
    )(zrest, zrest, ln_g.reshape(-1, 1, width), ln_b.reshape(-1, 1, width), w_s,
      b_s.reshape(b_s.shape[0], groups, chunk, 1))


def _router_body(h_ref, w_ref, b_ref, idx_ref, gate_ref, rank_ref, cnt_ref, carry_ref):
    i = pl.program_id(0)

    @pl.when(i == 0)
    def _():
        carry_ref[...] = jnp.zeros_like(carry_ref)

    logits = jnp.dot(h_ref[...], w_ref[...], preferred_element_type=_F32,
                     precision=lax.Precision.HIGHEST) + b_ref[...]
    tm, n_exp = logits.shape
    e_iota = lax.broadcasted_iota(_I32, (tm, n_exp), 1)
    k_iota = lax.broadcasted_iota(_I32, (tm, TOP_K), 1)
    work = logits
    sels, vals = [], []
    idx_out = jnp.zeros((tm, TOP_K), _I32)
    for k in range(TOP_K):
        m = jnp.max(work, axis=-1, keepdims=True)
        idx = jnp.min(jnp.where(work == m, e_iota, n_exp), axis=-1, keepdims=True)
        sel = e_iota == idx
        sels.append(sel)
        vals.append(m)
        idx_out = jnp.where(k_iota == k, idx, idx_out)
        work = jnp.where(sel, -jnp.inf, work)
    exps = [jnp.exp(v - vals[0]) for v in vals]
    denom = exps[0]
    for e in exps[1:]:
        denom = denom + e
    gate_out = jnp.zeros((tm, TOP_K), _F32)
    for k in range(TOP_K):
        gate_out = jnp.where(k_iota == k, exps[k] / denom, gate_out)

    onehot = jnp.zeros((tm, n_exp), _F32)
    for sel in sels:
        onehot = onehot + sel.astype(_F32)
    r_iota = lax.broadcasted_iota(_I32, (tm, tm), 0)
    c_iota = lax.broadcasted_iota(_I32, (tm, tm), 1)
    lower = (c_iota < r_iota).astype(_BF16)
    pos = jnp.dot(lower, onehot.astype(_BF16), preferred_element_type=_F32) + carry_ref[...]
    rank_out = jnp.zeros((tm, TOP_K), _I32)
    for k in range(TOP_K):
        rk = jnp.sum(jnp.where(sels[k], pos, 0.0), axis=-1, keepdims=True).astype(_I32)
        rank_out = jnp.where(k_iota == k, rk, rank_out)
    carry = carry_ref[...] + jnp.sum(onehot, axis=0, keepdims=True)
    carry_ref[...] = carry

    idx_ref[...] = idx_out
    gate_ref[...] = gate_out
    rank_ref[...] = rank_out
    cnt_ref[...] = carry.astype(_I32)


def _router(hf, w_router, b_router, layer):
    n, d = hf.shape
    n_exp = w_router.shape[2]
    tm = _tile(n, 512)
    vmem = 2 * _nbytes((tm, d), _F32) + 2 * _nbytes((d, V7X_LANES), _F32) + 6 * _nbytes((tm, tm), _F32)
    return pl.pallas_call(
        _router_body,
        grid=(n // tm,),
        in_specs=[
            pl.BlockSpec((tm, d), lambda i: (i, 0)),
            pl.BlockSpec((None, d, n_exp), lambda i: (layer, 0, 0)),
            pl.BlockSpec((None, 1, n_exp), lambda i: (layer, 0, 0)),
        ],
        out_specs=[
            pl.BlockSpec((tm, TOP_K), lambda i: (i, 0)),
            pl.BlockSpec((tm, TOP_K), lambda i: (i, 0)),
            pl.BlockSpec((tm, TOP_K), lambda i: (i, 0)),
            pl.BlockSpec((1, n_exp), lambda i: (0, 0)),
        ],
        out_shape=[
            jax.ShapeDtypeStruct((n, TOP_K), _I32),
            jax.ShapeDtypeStruct((n, TOP_K), _F32),
            jax.ShapeDtypeStruct((n, TOP_K), _I32),
            jax.ShapeDtypeStruct((1, n_exp), _I32),
        ],
        scratch_shapes=[pltpu.VMEM((1, n_exp), _F32)],
        compiler_params=_cparams(("arbitrary",), vmem + (8 << 20)),
        name="router",
    )(hf, w_router, b_router.reshape(b_router.shape[0], 1, n_exp))


def _dispatch_body(start_ref, pad_ref, idx_ref, rank_ref, hp_ref, xs_ref, zero_ref, sem):
    tm = hp_ref.shape[0]

    @pl.when(pl.program_id(0) == 0)
    def _():
        zero_ref[...] = jnp.zeros_like(zero_ref)
        n_zero = zero_ref.shape[0]
        copies = [pltpu.make_async_copy(
            zero_ref, xs_ref.at[pl.ds(pl.multiple_of(pad_ref[e], V7X_SUBLANES), n_zero)], sem)
            for e in range(pad_ref.shape[0])]
        for c in copies:
            c.start()
        for c in copies:
            c.wait()

    def row_copy(t, k):
        a = t * TOP_K + k
        dst = xs_ref.at[pl.ds(start_ref[idx_ref[a]] + rank_ref[a], 1)]
        return pltpu.make_async_copy(hp_ref.at[pl.ds(t, 1)], dst, sem)

    def start(t, c):
        for k in range(TOP_K):
            row_copy(t, k).start()
        return c

    def wait(t, c):
        for k in range(TOP_K):
            row_copy(t, k).wait()
        return c

    lax.fori_loop(0, tm, start, 0)
    lax.fori_loop(0, tm, wait, 0)


def _dispatch(hp, idx_flat, rank_flat, seg_start, seg_pad, n_slots):
    n, w = hp.shape
    tm = _tile(n, 256)
    return pl.pallas_call(
        _dispatch_body,
        grid_spec=pltpu.PrefetchScalarGridSpec(
            num_scalar_prefetch=2,
            grid=(n // tm,),
            in_specs=[
                pl.BlockSpec((tm * TOP_K,), lambda i, st, pd: (i,), memory_space=pltpu.SMEM),
                pl.BlockSpec((tm * TOP_K,), lambda i, st, pd: (i,), memory_space=pltpu.SMEM),
                pl.BlockSpec((tm, w), lambda i, st, pd: (i, 0)),
            ],
            out_specs=pl.BlockSpec(memory_space=pl.ANY),
            scratch_shapes=[pltpu.VMEM((MOE_ZERO_ROWS, w), hp.dtype), pltpu.SemaphoreType.DMA(())],
        ),
        out_shape=jax.ShapeDtypeStruct((n_slots + MOE_ZERO_ROWS, w), hp.dtype),
        compiler_params=_cparams(("arbitrary",), 4 * _nbytes((tm, w), hp.dtype) + (4 << 20)),
        name="moe_dispatch",
    )(seg_start, seg_pad, idx_flat, rank_flat, hp)


def _for_each_chunk(rows, chunk, fn):
    rows_p = ((rows + (MOE_TAIL - 1)) // MOE_TAIL) * MOE_TAIL
    n_full = rows_p // chunk

    def full(c, carry):
        fn(pl.multiple_of(c * chunk, chunk), chunk)
        return carry

    lax.fori_loop(0, n_full, full, 0)
    rem = rows_p - n_full * chunk
    size = chunk // 2
    while size >= MOE_TAIL:
        def piece(size=size):
            off = n_full * chunk + (rem - rem % (2 * size))
            fn(pl.multiple_of(off, size), size)
        pl.when((rem // size) % 2 == 1)(piece)
        size //= 2


def _unpack_halves(words):
    lo_cols = lax.bitcast_convert_type(words & jnp.uint32(0xFFFF0000), _F32)
    hi_cols = lax.bitcast_convert_type(words << 16, _F32)
    return lo_cols, hi_cols


def _gmm_gate_up_body(sbe_ref, blk_ref, rows_ref, nlive_ref, x_ref, wg_ref, wu_ref, bg_ref, bu_ref, o_ref):
    del sbe_ref, blk_ref, nlive_ref
    s = pl.program_id(0)
    half = x_ref.shape[1]

    def chunk(r0, size):
        xa, xb = _unpack_halves(x_ref[pl.ds(r0, size), :])
        g = (jnp.dot(xa, wg_ref[:half, :], preferred_element_type=_F32)
             + jnp.dot(xb, wg_ref[half:, :], preferred_element_type=_F32) + bg_ref[...])
        u = (jnp.dot(xa, wu_ref[:half, :], preferred_element_type=_F32)
             + jnp.dot(xb, wu_ref[half:, :], preferred_element_type=_F32) + bu_ref[...])
        g = jnp.minimum(g, SWIGLU_LIMIT)
        u = jnp.clip(u, -SWIGLU_LIMIT, SWIGLU_LIMIT)
        act = (u + 1.0) * (g * _sigmoid(g * SWIGLU_ALPHA))
        o_ref[pl.ds(r0, size), :] = act.astype(o_ref.dtype)

    _for_each_chunk(rows_ref[s], MOE_CHUNK_GATE_UP, chunk)


def _gmm_down_body(sbe_ref, blk_ref, rows_ref, nlive_ref, a_ref, w_ref, b_ref, o_ref):
    del sbe_ref, blk_ref, nlive_ref
    s = pl.program_id(0)

    def chunk(r0, size):
        y = jnp.dot(a_ref[pl.ds(r0, size), :], w_ref[...], preferred_element_type=_F32) + b_ref[...]
        o_ref[pl.ds(r0, size), :] = y

    _for_each_chunk(rows_ref[s], MOE_CHUNK_DOWN, chunk)


def _expert_mlp(xs, meta, w_gu, b_gu, w_down, b_down, layer, cap):
    sbe, blk, sb_rows, n_live = meta
    n_sb = sbe.shape[0]
    n_slots, half = xs.shape
    d = 2 * half
    n_exp = w_gu.shape[1]
    d_ff = w_gu.shape[3] // 2
    tf = _tile(d_ff, 256)
    nf = d_ff // tf
    b_gu4 = b_gu.reshape(b_gu.shape[0], n_exp, 1, 2 * d_ff)

    def col(s, j, nl, n_tiles):
        return jnp.where(s < nl[0], j, n_tiles - 1)

    vmem1 = (_nbytes((cap, half), _U32) + 4 * _nbytes((d, tf), _F32) + 2 * _nbytes((cap, tf), _BF16)
             + _nbytes((MOE_CHUNK_GATE_UP, d), _F32) + 6 * _nbytes((MOE_CHUNK_GATE_UP, tf), _F32))
    act = pl.pallas_call(
        _gmm_gate_up_body,
        grid_spec=pltpu.PrefetchScalarGridSpec(
            num_scalar_prefetch=4,
            grid=(n_sb, nf),
            in_specs=[
                pl.BlockSpec((cap, half), lambda s, j, e, b, r, nl: (b[s], 0),
                             pipeline_mode=pl.Buffered(1)),
                pl.BlockSpec((None, None, d, tf), lambda s, j, e, b, r, nl: (layer, e[s], 0, col(s, j, nl, nf))),
                pl.BlockSpec((None, None, d, tf), lambda s, j, e, b, r, nl: (layer, e[s], 0, nf + col(s, j, nl, nf))),
                pl.BlockSpec((None, None, 1, tf), lambda s, j, e, b, r, nl: (layer, e[s], 0, col(s, j, nl, nf))),
                pl.BlockSpec((None, None, 1, tf), lambda s, j, e, b, r, nl: (layer, e[s], 0, nf + col(s, j, nl, nf))),
            ],
            out_specs=pl.BlockSpec((cap, tf), lambda s, j, e, b, r, nl: (b[s], col(s, j, nl, nf))),
        ),
        out_shape=jax.ShapeDtypeStruct((n_slots, d_ff), _BF16),
        compiler_params=_cparams(("arbitrary", "arbitrary"), vmem1 + (4 << 20)),
        name="moe_gate_up",
    )(sbe, blk, sb_rows, n_live, xs, w_gu, w_gu, b_gu4, b_gu4)

    tn = _tile(d, 512)
    nn = d // tn
    vmem2 = (_nbytes((cap, d_ff), _BF16) + 2 * _nbytes((d_ff, tn), _F32) + 2 * _nbytes((cap, tn), _F32)
             + 3 * _nbytes((MOE_CHUNK_DOWN, tn), _F32))
    ys = pl.pallas_call(
        _gmm_down_body,
        grid_spec=pltpu.PrefetchScalarGridSpec(
            num_scalar_prefetch=4,
            grid=(n_sb, nn),
            in_specs=[
                pl.BlockSpec((cap, d_ff), lambda s, j, e, b, r, nl: (b[s], 0),
                             pipeline_mode=pl.Buffered(1)),
                pl.BlockSpec((None, None, d_ff, tn), lambda s, j, e, b, r, nl: (layer, e[s], 0, col(s, j, nl, nn))),
                pl.BlockSpec((None, None, 1, tn), lambda s, j, e, b, r, nl: (layer, e[s], 0, col(s, j, nl, nn))),
            ],
            out_specs=pl.BlockSpec((cap, tn), lambda s, j, e, b, r, nl: (b[s], col(s, j, nl, nn))),
        ),
        out_shape=jax.ShapeDtypeStruct((n_slots, d), _F32),
        compiler_params=_cparams(("arbitrary", "arbitrary"), vmem2 + (4 << 20)),
        name="moe_down",
    )(sbe, blk, sb_rows, n_live, act, w_down, b_down.reshape(b_down.shape[0], n_exp, 1, d))
    return ys


def _combine_ln_body(start_ref, idx_ref, rank_ref, idx_nxt_ref, rank_nxt_ref, gate_ref, h_ref, e_ref,
                     g_ref, b_ref, ys_ref, o_ref, buf, sem, *, alpha, n_steps):
    i = pl.program_id(0)
    tm = gate_ref.shape[0]
    slot = i % 2

    def row_copy(src_row, slot_, t, k):
        return pltpu.make_async_copy(ys_ref.at[pl.ds(src_row, 1)], buf.at[slot_, k, pl.ds(t, 1)],
                                     sem.at[slot_])

    def gather(ids_ref, rks_ref, slot_):
        def body(t, c):
            for k in range(TOP_K):
                a = t * TOP_K + k
                row_copy(start_ref[ids_ref[a]] + rks_ref[a], slot_, t, k).start()
            return c
        lax.fori_loop(0, tm, body, 0)

    @pl.when(i == 0)
    def _():
        gather(idx_ref, rank_ref, 0)

    @pl.when(i + 1 < n_steps)
    def _():
        gather(idx_nxt_ref, rank_nxt_ref, 1 - slot)

    def wait(t, c):
        for k in range(TOP_K):
            row_copy(0, slot, t, k).wait()
        return c

    lax.fori_loop(0, tm, wait, 0)
    gate = gate_ref[...]
    acc = alpha * h_ref[...] + e_ref[...]
    for k in range(TOP_K):
        acc = acc + gate[:, k:k + 1] * buf[slot, k]
    o_ref[...] = _layer_norm_rows(acc, g_ref[...], b_ref[...])


def _combine_ln(ys, idx_flat, rank_flat, seg_start, gate, hf, e, g, b, alpha, row_off, rows):
    d = ys.shape[1]
    tm = _tile(rows, 128)
    n_steps = rows // tm
    off = row_off // tm
    assert row_off % tm == 0

    def cur(i, st):
        return (i + off,)

    def nxt(i, st):
        return (jnp.minimum(i + 1, n_steps - 1) + off,)

    blk = _nbytes((tm, d), _F32)
    return pl.pallas_call(
        functools.partial(_combine_ln_body, alpha=alpha, n_steps=n_steps),
        grid_spec=pltpu.PrefetchScalarGridSpec(
            num_scalar_prefetch=1,
            grid=(n_steps,),
            in_specs=[
                pl.BlockSpec((tm * TOP_K,), cur, memory_space=pltpu.SMEM),
                pl.BlockSpec((tm * TOP_K,), cur, memory_space=pltpu.SMEM),
                pl.BlockSpec((tm * TOP_K,), nxt, memory_space=pltpu.SMEM),
                pl.BlockSpec((tm * TOP_K,), nxt, memory_space=pltpu.SMEM),
                pl.BlockSpec((tm, TOP_K), lambda i, st: (i + off, 0)),
                pl.BlockSpec((tm, d), lambda i, st: (i + off, 0)),
                pl.BlockSpec((tm, d), lambda i, st: (i + off, 0)),
                pl.BlockSpec((1, d), lambda i, st: (0, 0)),
                pl.BlockSpec((1, d), lambda i, st: (0, 0)),
                pl.BlockSpec(memory_space=pl.ANY),
            ],
            out_specs=pl.BlockSpec((tm, d), lambda i, st: (i, 0)),
            scratch_shapes=[pltpu.VMEM((2, TOP_K, tm, d), _F32), pltpu.SemaphoreType.DMA((2,))],
        ),
        out_shape=jax.ShapeDtypeStruct((rows, d), _F32),
        compiler_params=_cparams(("arbitrary",), (2 * TOP_K + 10) * blk + (4 << 20)),
        name="moe_combine_ln",
    )(seg_start, idx_flat, rank_flat, idx_flat, rank_flat, gate, hf, e, g.reshape(1, d), b.reshape(1, d), ys)


def _moe_capacity(n_assign, n_exp):
    mean = -(-n_assign // n_exp)
    return -(-(mean + mean // 8) // MOE_TAIL) * MOE_TAIL


def _moe_plan(counts, cap, n_sb):
    n_exp = counts.shape[0]
    sb_per_e = (counts + cap - 1) // cap
    sb_end = jnp.cumsum(sb_per_e)
    sb_start = sb_end - sb_per_e
    n_live = sb_end[-1]
    sidx = jnp.arange(n_sb, dtype=_I32)
    e_of = jnp.minimum(jnp.sum(sidx[:, None] >= sb_end[None, :], axis=1), n_exp - 1).astype(_I32)
    of_e = e_of[:, None] == jnp.arange(n_exp, dtype=_I32)[None, :]
    local = sidx - jnp.sum(jnp.where(of_e, sb_start[None, :], 0), axis=1)
    cnt = jnp.sum(jnp.where(of_e, counts[None, :], 0), axis=1)
    live = sidx < n_live
    rows = jnp.where(live, jnp.clip(cnt - local * cap, 0, cap), 0).astype(_I32)
    last = jnp.maximum(n_live - 1, 0)
    e_last = jnp.sum(jnp.where(sidx == last, e_of, 0))
    sbe = jnp.where(live, e_of, e_last).astype(_I32)
    blk = jnp.where(live, sidx, last).astype(_I32)
    seg_start = (sb_start * cap).astype(_I32)
    meta = (sbe, blk, rows, n_live.reshape(1).astype(_I32))
    seg_pad = (seg_start + counts) // V7X_SUBLANES * V7X_SUBLANES
    return meta, seg_start, seg_pad.astype(_I32)


def _moe_experts(hf, hp, w_router, b_router, w_gu, b_gu, w_down, b_down, layer):
    n = hf.shape[0]
    n_exp = w_router.shape[2]
    top_idx, gate, rank, counts = _router(hf, w_router, b_router, layer)
    n_assign = n * TOP_K
    cap = _moe_capacity(n_assign, n_exp)
    n_sb = n_assign // cap + n_exp
    meta, seg_start, seg_pad = _moe_plan(counts[0], cap, n_sb)
    idx_flat, rank_flat = top_idx.reshape(-1), rank.reshape(-1)
    xs = _dispatch(hp, idx_flat, rank_flat, seg_start, seg_pad, n_sb * cap)
    ys = _expert_mlp(xs, meta, w_gu, b_gu, w_down, b_down, layer, cap)
    return ys, (idx_flat, rank_flat, seg_start, gate)


def kernel(x_prompt, x_sample, p_prompt, p_sample, ln_emb_g, ln_emb_b, w_in, b_in, rpb, sgu_ln_g, sgu_ln_b, sgu_w, sgu_b, w_pa, w_pb, w_o, ln1_g, ln1_b, w_router, b_router, w_gu, b_gu, w_down, b_down, w_pg, w_ple, ln2_g, ln2_b):
    depth = w_in.shape[0]
    d = x_prompt.shape[-1]
    ba, ta = x_prompt.shape[:2]
    bb, tb = x_sample.shape[:2]
    na_rows, nb_rows = ba * ta, bb * tb
    na_width = w_pa.shape[1]
    sgu_width = w_pb.shape[1]
    heads = na_width // NA_HEAD_DIM
    alpha = (2.0 * depth) ** 0.25
    qkv_cols = 3 * na_width

    xf, xb = _ln_emb(x_prompt.reshape(na_rows, d), x_sample.reshape(nb_rows, d), ln_emb_g, ln_emb_b)
    p_all = jnp.concatenate([p_prompt.reshape(depth, na_rows, -1), p_sample.reshape(depth, nb_rows, -1)], axis=1)

    for l in range(depth):
        zqkv = _in_proj(xb, w_in, b_in, l, 0, qkv_cols, _BF16)
        zrest = _in_proj(xb, w_in, b_in, l, qkv_cols, w_in.shape[2] - qkv_cols, _F32)
        bias_tab = _na_bias_table(rpb[l])
        na_out = jnp.concatenate([
            _neighbourhood_attention(zqkv, bias_tab, 0, ba, ta, heads),
            _neighbourhood_attention(zqkv, bias_tab, na_rows, bb, tb, heads)], axis=0)
        sgu_out = _spatial_gating(zrest, sgu_ln_g, sgu_ln_b, sgu_w, sgu_b, l, 0, sgu_width, sgu_width)
        merged = _merge(na_out, sgu_out, w_pa, w_pb, zrest, l, 2 * sgu_width, 2 * sgu_width + d)
        s1 = _out_proj(merged, w_o, xf, l, alpha)
        hf, hb, hp = _ln_mid(s1, ln1_g[l], ln1_b[l])
        ys, (idx_flat, rank_flat, seg_start, gate) = _moe_experts(
            hf, hp, w_router, b_router, w_gu, b_gu, w_down, b_down, l)
        e = _ple_gate(hb, w_pg, p_all[l], w_ple, l)

        def layer_out(row_off, rows, l=l, ys=ys, e=e, hf=hf):
            return _combine_ln(ys, idx_flat, rank_flat, seg_start, gate, hf, e, ln2_g[l], ln2_b[l],
                               alpha, row_off, rows)

        if l + 1 < depth:
            xf = layer_out(0, na_rows + nb_rows)
            xb = xf.astype(_BF16)

    y_a = layer_out(0, na_rows)
    y_b = layer_out(na_rows, nb_rows)
    return (y_a.reshape(ba, ta, d), y_b.reshape(bb, tb, d))
```

```python
import functools

import jax
import jax.numpy as jnp
import numpy as np
from jax import lax
from jax.experimental import pallas as pl
from jax.experimental.pallas import tpu as pltpu

_F32 = jnp.float32
_BF16 = jnp.bfloat16
_I32 = jnp.int32
_U32 = jnp.uint32

LN_EPS = 1e-5
GRID_W = 64
WIN_R = 8
WIN_C = 16
NA_HEAD_DIM = 128
TOP_K = 4
SWIGLU_ALPHA = 1.702
SWIGLU_LIMIT = 7.0
MASK_VALUE = -1e30
NA_BLOCK_ROWS = 4
NA_KEY_ROWS = 12
NA_UNROLL = 4

V7X_VMEM_BYTES = 64 * 1024 * 1024
V7X_LANES = 128
V7X_SUBLANES = 8
V7X_MXU_DIM = 256
MOE_CHUNK_GATE_UP = 2 * V7X_MXU_DIM
MOE_CHUNK_DOWN = 4 * V7X_MXU_DIM
MOE_TAIL = V7X_MXU_DIM // 2
MOE_ZERO_ROWS = MOE_TAIL + V7X_SUBLANES
DMA_WAIT_UNROLL = 8
COMBINE_GROUP = 4 * V7X_SUBLANES


def _cparams(semantics, vmem_bytes):
    limit = min(int(vmem_bytes), V7X_VMEM_BYTES - 4 * 1024 * 1024)
    return pltpu.CompilerParams(dimension_semantics=semantics, vmem_limit_bytes=limit)


def _nbytes(shape, dtype):
    n = 1
    for s in shape:
        n *= s
    return n * jnp.dtype(dtype).itemsize


def _tile(dim, pref):
    t = min(dim, pref)
    while dim % t:
        t //= 2
    return t


def _layer_norm_rows(x, g, b):
    mu = jnp.mean(x, axis=-1, keepdims=True)
    xc = x - mu
    var = jnp.mean(xc * xc, axis=-1, keepdims=True)
    return xc * lax.rsqrt(var + LN_EPS) * g + b


def _gelu(x):
    return 0.5 * x * (1.0 + lax.erf(x * 0.7071067811865476))


def _sigmoid(x):
    return 1.0 / (1.0 + jnp.exp(-x))


def _ln_emb_body(xa_ref, xb_ref, g_ref, b_ref, of_ref, ob_ref, *, na):
    i = pl.program_id(0)

    def emit(x_ref):
        y = _layer_norm_rows(x_ref[...], g_ref[...], b_ref[...])
        of_ref[...] = y
        ob_ref[...] = y.astype(_BF16)

    @pl.when(i < na)
    def _():
        emit(xa_ref)

    @pl.when(i >= na)
    def _():
        emit(xb_ref)


def _ln_emb(xa, xb, g, b):
    d = xa.shape[1]
    tm = _tile(min(xa.shape[0], xb.shape[0]), 256)
    na, nb = xa.shape[0] // tm, xb.shape[0] // tm
    n = xa.shape[0] + xb.shape[0]
    blk = _nbytes((tm, d), _F32)
    return pl.pallas_call(
        functools.partial(_ln_emb_body, na=na),
        grid=(na + nb,),
        in_specs=[
            pl.BlockSpec((tm, d), lambda i: (jnp.minimum(i, na - 1), 0)),
            pl.BlockSpec((tm, d), lambda i: (jnp.maximum(i - na, 0), 0)),
            pl.BlockSpec((1, d), lambda i: (0, 0)),
            pl.BlockSpec((1, d), lambda i: (0, 0)),
        ],
        out_specs=[pl.BlockSpec((tm, d), lambda i: (i, 0)),
                   pl.BlockSpec((tm, d), lambda i: (i, 0))],
        out_shape=[jax.ShapeDtypeStruct((n, d), _F32), jax.ShapeDtypeStruct((n, d), _BF16)],
        compiler_params=_cparams(("arbitrary",), 12 * blk),
        name="ln_emb",
    )(xa, xb, g.reshape(1, d), b.reshape(1, d))


def _ln_route_body(s_ref, g_ref, b_ref, wr_hi_ref, wr_lo_ref, br_ref, hf_ref, hb_ref, hp_ref, idx_ref,
                   gate_ref, rank_ref, cnt_ref, carry_ref):
    y = _layer_norm_rows(s_ref[...], g_ref[...], b_ref[...])
    hf_ref[...] = y
    yb = y.astype(_BF16)
    hb_ref[...] = yb
    bits = lax.bitcast_convert_type(yb.astype(_F32), _U32)
    half = bits.shape[1] // 2
    hp_ref[...] = bits[:, :half] | (bits[:, half:] >> 16)

    @pl.when(pl.program_id(0) == 0)
    def _():
        carry_ref[...] = jnp.zeros_like(carry_ref)

    y_lo = (y - yb.astype(_F32)).astype(_BF16)
    logits = (jnp.dot(yb, wr_hi_ref[...], preferred_element_type=_F32)
              + jnp.dot(yb, wr_lo_ref[...], preferred_element_type=_F32)
              + jnp.dot(y_lo, wr_hi_ref[...], preferred_element_type=_F32) + br_ref[...])
    idx, gate, rank, carry = _route(logits, carry_ref[...])
    carry_ref[...] = carry
    idx_ref[...] = idx
    gate_ref[...] = gate
    rank_ref[...] = rank
    cnt_ref[...] = carry.astype(_I32)


def _ln_route(s, g, b, w_router, b_router, layer):
    n, d = s.shape
    n_exp = w_router.shape[2]
    tm = _tile(n, 256)
    blk = _nbytes((tm, d), _F32)
    row = lambda i: (i, 0)
    w_hi = w_router[layer].astype(_BF16)
    w_lo = (w_router[layer] - w_hi.astype(_F32)).astype(_BF16)
    return pl.pallas_call(
        _ln_route_body,
        grid=(n // tm,),
        in_specs=[pl.BlockSpec((tm, d), row),
                  pl.BlockSpec((1, d), lambda i: (0, 0)),
                  pl.BlockSpec((1, d), lambda i: (0, 0)),
                  pl.BlockSpec((d, n_exp), lambda i: (0, 0)),
                  pl.BlockSpec((d, n_exp), lambda i: (0, 0)),
                  pl.BlockSpec((None, 1, n_exp), lambda i: (layer, 0, 0))],
        out_specs=[pl.BlockSpec((tm, d), row),
                   pl.BlockSpec((tm, d), row),
                   pl.BlockSpec((tm, d // 2), row),
                   pl.BlockSpec((tm, TOP_K), row),
                   pl.BlockSpec((tm, TOP_K), row),
                   pl.BlockSpec((tm, TOP_K), row),
                   pl.BlockSpec((1, n_exp), lambda i: (0, 0))],
        out_shape=[jax.ShapeDtypeStruct((n, d), _F32),
                   jax.ShapeDtypeStruct((n, d), _BF16),
                   jax.ShapeDtypeStruct((n, d // 2), _U32),
                   jax.ShapeDtypeStruct((n, TOP_K), _I32),
                   jax.ShapeDtypeStruct((n, TOP_K), _F32),
                   jax.ShapeDtypeStruct((n, TOP_K), _I32),
                   jax.ShapeDtypeStruct((1, n_exp), _I32)],
        scratch_shapes=[pltpu.VMEM((1, n_exp), _F32)],
        compiler_params=_cparams(("arbitrary",), 14 * blk + 2 * _nbytes((d, V7X_LANES), _F32)),
        name="ln_route",
    )(s, g.reshape(1, d), b.reshape(1, d), w_hi, w_lo, b_router.reshape(b_router.shape[0], 1, n_exp))


def _in_proj_body(a_ref, w_ref, b_ref, o_ref):
    acc = jnp.dot(a_ref[...], w_ref[...], preferred_element_type=_F32)
    o_ref[...] = (acc + b_ref[...]).astype(o_ref.dtype)


def _in_proj(a, w, bias, layer, col_off, cols, out_dtype):
    m, k = a.shape
    bm = _tile(m, 2048)
    bn = _tile(cols, 512)
    joff = col_off // bn
    bias3 = bias.reshape(bias.shape[0], 1, bias.shape[1])
    vmem = (_nbytes((bm, k), a.dtype) + 2 * _nbytes((k, bn), w.dtype)
            + 2 * _nbytes((bm, bn), out_dtype) + 2 * _nbytes((bm, bn), _F32))
    return pl.pallas_call(
        _in_proj_body,
        grid=(m // bm, cols // bn),
        in_specs=[
            pl.BlockSpec((bm, k), lambda i, j: (i, 0), pipeline_mode=pl.Buffered(1)),
            pl.BlockSpec((None, k, bn), lambda i, j: (layer, 0, j + joff)),
            pl.BlockSpec((None, 1, bn), lambda i, j: (layer, 0, j + joff)),
        ],
        out_specs=pl.BlockSpec((bm, bn), lambda i, j: (i, j)),
        out_shape=jax.ShapeDtypeStruct((m, cols), out_dtype),
        compiler_params=_cparams(("arbitrary", "arbitrary"), vmem + (4 << 20)),
        name="in_proj",
    )(a, w, bias3)


def _merge_body(na_ref, sg_ref, wa_ref, wb_ref, ga_ref, gb_ref, o_ref):
    ya = jnp.dot(na_ref[...], wa_ref[...], preferred_element_type=_F32)
    yb = jnp.dot(sg_ref[...], wb_ref[...], preferred_element_type=_F32)
    merged = _sigmoid(ga_ref[...]) * ya + _sigmoid(gb_ref[...]) * yb
    o_ref[...] = merged.astype(o_ref.dtype)


def _merge(na_out, sgu_out, w_pa, w_pb, zrest, layer, ga_col, gb_col):
    m, ka = na_out.shape
    kb = sgu_out.shape[1]
    d = w_pa.shape[2]
    bm = _tile(m, 1024)
    bn = _tile(d, 512)
    ja, jb = ga_col // bn, gb_col // bn
    vmem = 2 * (_nbytes((bm, ka), _BF16) + _nbytes((bm, kb), _BF16) + _nbytes((ka, bn), _F32)
                + _nbytes((kb, bn), _F32) + 2 * _nbytes((bm, bn), _F32) + _nbytes((bm, bn), _BF16))
    vmem += 3 * _nbytes((bm, bn), _F32)
    return pl.pallas_call(
        _merge_body,
        grid=(m // bm, d // bn),
        in_specs=[
            pl.BlockSpec((bm, ka), lambda i, j: (i, 0)),
            pl.BlockSpec((bm, kb), lambda i, j: (i, 0)),
            pl.BlockSpec((None, ka, bn), lambda i, j: (layer, 0, j)),
            pl.BlockSpec((None, kb, bn), lambda i, j: (layer, 0, j)),
            pl.BlockSpec((bm, bn), lambda i, j: (i, j + ja)),
            pl.BlockSpec((bm, bn), lambda i, j: (i, j + jb)),
        ],
        out_specs=pl.BlockSpec((bm, bn), lambda i, j: (i, j)),
        out_shape=jax.ShapeDtypeStruct((m, d), _BF16),
        compiler_params=_cparams(("arbitrary", "arbitrary"), vmem + (4 << 20)),
        name="merge",
    )(na_out, sgu_out, w_pa, w_pb, zrest, zrest)


def _out_proj_body(a_ref, w_ref, x_ref, o_ref, *, alpha):
    acc = jnp.dot(a_ref[...], w_ref[...], preferred_element_type=_F32)
    o_ref[...] = alpha * x_ref[...] + acc


def _out_proj(a, w, resid, layer, alpha):
    m, k = a.shape
    d = w.shape[2]
    bm = _tile(m, 1024)
    bn = _tile(d, 512)
    vmem = 2 * (_nbytes((bm, k), a.dtype) + _nbytes((k, bn), w.dtype) + 2 * _nbytes((bm, bn), _F32))
    vmem += 2 * _nbytes((bm, bn), _F32)
    return pl.pallas_call(
        functools.partial(_out_proj_body, alpha=alpha),
        grid=(m // bm, d // bn),
        in_specs=[
            pl.BlockSpec((bm, k), lambda i, j: (i, 0)),
            pl.BlockSpec((None, k, bn), lambda i, j: (layer, 0, j)),
            pl.BlockSpec((bm, bn), lambda i, j: (i, j)),
        ],
        out_specs=pl.BlockSpec((bm, bn), lambda i, j: (i, j)),
        out_shape=jax.ShapeDtypeStruct((m, d), _F32),
        compiler_params=_cparams(("arbitrary", "arbitrary"), vmem + (4 << 20)),
        name="out_proj",
    )(a, w, resid)


def _ple_gate_body(a_ref, w_ref, p_ref, wp_ref, o_ref):
    gate = _sigmoid(jnp.dot(a_ref[...], w_ref[...], preferred_element_type=_F32))
    ple = jnp.dot(p_ref[...], wp_ref[...], preferred_element_type=_F32)
    o_ref[...] = gate * ple


def _ple_gate(hb, w_pg, p, w_ple, layer):
    m, k = hb.shape
    d = w_pg.shape[2]
    kp = p.shape[1]
    bm = _tile(m, 2048)
    bn = _tile(d, 512)
    vmem = (_nbytes((bm, k), hb.dtype) + _nbytes((bm, kp), _F32)
            + 2 * (_nbytes((k, bn), w_pg.dtype) + _nbytes((kp, bn), _F32) + _nbytes((bm, bn), _F32))
            + 3 * _nbytes((bm, bn), _F32))
    return pl.pallas_call(
        _ple_gate_body,
        grid=(m // bm, d // bn),
        in_specs=[
            pl.BlockSpec((bm, k), lambda i, j: (i, 0), pipeline_mode=pl.Buffered(1)),
            pl.BlockSpec((None, k, bn), lambda i, j: (layer, 0, j)),
            pl.BlockSpec((bm, kp), lambda i, j: (i, 0), pipeline_mode=pl.Buffered(1)),
            pl.BlockSpec((None, kp, bn), lambda i, j: (layer, 0, j)),
        ],
        out_specs=pl.BlockSpec((bm, bn), lambda i, j: (i, j)),
        out_shape=jax.ShapeDtypeStruct((m, d), _F32),
        compiler_params=_cparams(("arbitrary", "arbitrary"), vmem + (4 << 20)),
        name="ple_gate",
    )(hb, w_pg, p, w_ple)


def _na_bias_table(rpb_l):
    heads = rpb_l.shape[0]
    half = WIN_R // 2
    i = np.arange(NA_BLOCK_ROWS)
    win_off = np.stack([np.zeros_like(i), i, np.full_like(i, NA_KEY_ROWS - WIN_R)])
    win_d = np.stack([i, np.full_like(i, half), half + i])
    kr = np.arange(NA_KEY_ROWS)
    j = kr[None, None, :] - win_off[:, :, None]
    row_ok = (j >= 0) & (j < WIN_R)
    dr = j - win_d[:, :, None] + (WIN_R - 1)
    row_sel = row_ok[..., None] & (dr[..., None] == np.arange(2 * WIN_R - 1))
    cols = np.arange(GRID_W)
    col_start = np.clip(cols - WIN_C // 2, 0, GRID_W - WIN_C)
    col_ok = (cols[None, :] >= col_start[:, None]) & (cols[None, :] < col_start[:, None] + WIN_C)
    dc = cols[None, :] - cols[:, None] + (WIN_C - 1)
    col_sel = col_ok[..., None] & (dc[..., None] == np.arange(2 * WIN_C - 1))
    tab = jnp.einsum("hab,pika,qcb->hpiqkc", rpb_l.astype(_F32), row_sel.astype(np.float32),
                     col_sel.astype(np.float32), precision=lax.Precision.HIGHEST)
    keep = row_ok[:, :, None, :, None] & col_ok[None, None, :, None, :]
    tab = jnp.where(keep[None], tab, MASK_VALUE)
    return tab.reshape(heads, 3, NA_BLOCK_ROWS * GRID_W, NA_KEY_ROWS * GRID_W)


def _na_body(q_ref, k_ref, v_ref, bias_ref, o_ref, *, rows, scale):
    n_q = NA_BLOCK_ROWS * GRID_W
    n_k = NA_KEY_ROWS * GRID_W
    n_blocks = rows // NA_BLOCK_ROWS

    def block_fn(ib, carry):
        r0 = ib * NA_BLOCK_ROWS
        kb0 = jnp.clip(r0 - WIN_R // 2, 0, rows - NA_KEY_ROWS)
        pattern = jnp.where(ib == 0, 0, jnp.where(ib == n_blocks - 1, 2, 1))
        q0 = pl.multiple_of(r0 * GRID_W, n_q)
        k0 = pl.multiple_of(kb0 * GRID_W, NA_BLOCK_ROWS * GRID_W)
        q = q_ref[pl.ds(q0, n_q), :]
        kw = k_ref[pl.ds(k0, n_k), :]
        vw = v_ref[pl.ds(k0, n_k), :]
        s = lax.dot_general(q, kw, (((1,), (1,)), ((), ())), preferred_element_type=_F32)
        s = s * scale + bias_ref[pattern]
        m = jnp.max(s, axis=-1, keepdims=True)
        e = jnp.exp(s - m)
        p = e / jnp.sum(e, axis=-1, keepdims=True)
        o = jnp.dot(p.astype(_BF16), vw, preferred_element_type=_F32)
        o_ref[pl.ds(q0, n_q), :] = o.astype(o_ref.dtype)
        return carry

    lax.fori_loop(0, n_blocks, block_fn, 0, unroll=NA_UNROLL)


def _neighbourhood_attention(zqkv, bias_tab, row_off, batch, seq, heads):
    assert seq % GRID_W == 0 and row_off % seq == 0
    rows = seq // GRID_W
    assert rows % NA_BLOCK_ROWS == 0 and rows >= NA_KEY_ROWS
    boff = row_off // seq
    dh = NA_HEAD_DIM
    n_q = NA_BLOCK_ROWS * GRID_W
    n_k = NA_KEY_ROWS * GRID_W
    vmem = 2 * (4 * _nbytes((seq, dh), _BF16) + _nbytes((3, n_q, n_k), _F32)) + 6 * _nbytes((n_q, n_k), _F32)
    return pl.pallas_call(
        functools.partial(_na_body, rows=rows, scale=dh ** -0.5),
        grid=(heads, batch),
        in_specs=[
            pl.BlockSpec((seq, dh), lambda h, b: (b + boff, h)),
            pl.BlockSpec((seq, dh), lambda h, b: (b + boff, heads + h)),
            pl.BlockSpec((seq, dh), lambda h, b: (b + boff, 2 * heads + h)),
            pl.BlockSpec((None, 3, n_q, n_k), lambda h, b: (h, 0, 0, 0)),
        ],
        out_specs=pl.BlockSpec((seq, dh), lambda h, b: (b, h)),
        out_shape=jax.ShapeDtypeStruct((batch * seq, heads * dh), _BF16),
        compiler_params=_cparams(("arbitrary", "arbitrary"), vmem),
        name="neighbourhood_attention",
    )(zqkv, zqkv, zqkv, bias_tab)


def _sgu_body(u_ref, v_ref, g_ref, b_ref, ws_ref, bs_ref, o_ref, *, groups):
    u = _gelu(u_ref[...])
    vn = _layer_norm_rows(_gelu(v_ref[...]), g_ref[...], b_ref[...])
    gw = u.shape[1] // groups
    for g in range(groups):
        sl = slice(g * gw, (g + 1) * gw)
        s = jnp.dot(ws_ref[g], vn[:, sl], preferred_element_type=_F32) + bs_ref[g]
        o_ref[:, sl] = (u[:, sl] * s).astype(o_ref.dtype)


def _spatial_gating(zrest, ln_g, ln_b, w_s, b_s, layer, u_col, v_col, width):
    n = zrest.shape[0]
    groups, chunk = w_s.shape[1], w_s.shape[2]
    ju, jv = u_col // width, v_col // width
    blk = _nbytes((chunk, width), _F32)
    return pl.pallas_call(
        functools.partial(_sgu_body, groups=groups),
        grid=(n // chunk,),
        in_specs=[
            pl.BlockSpec((chunk, width), lambda i: (i, ju)),
            pl.BlockSpec((chunk, width), lambda i: (i, jv)),
            pl.BlockSpec((None, 1, width), lambda i: (layer, 0, 0)),
            pl.BlockSpec((None, 1, width), lambda i: (layer, 0, 0)),
            pl.BlockSpec((None, groups, chunk, chunk), lambda i: (layer, 0, 0, 0)),
            pl.BlockSpec((None, groups, chunk, 1), lambda i: (layer, 0, 0, 0)),
        ],
        out_specs=pl.BlockSpec((chunk, width), lambda i: (i, 0)),
        out_shape=jax.ShapeDtypeStruct((n, width), _BF16),
        compiler_params=_cparams(("arbitrary",), 12 * blk + (4 << 20)),
        name="spatial_gating",
    )(zrest, zrest, ln_g.reshape(-1, 1, width), ln_b.reshape(-1, 1, width), w_s,
      b_s.reshape(b_s.shape[0], groups, chunk, 1))


def _route(logits, carry):
    tm, n_exp = logits.shape
    e_iota = lax.broadcasted_iota(_I32, (tm, n_exp), 1)
    k_iota = lax.broadcasted_iota(_I32, (tm, TOP_K), 1)
    work = logits
    sels, vals = [], []
    idx_out = jnp.zeros((tm, TOP_K), _I32)
    for k in range(TOP_K):
        m = jnp.max(work, axis=-1, keepdims=True)
        idx = jnp.min(jnp.where(work == m, e_iota, n_exp), axis=-1, keepdims=True)
        sel = e_iota == idx
        sels.append(sel)
        vals.append(m)
        idx_out = jnp.where(k_iota == k, idx, idx_out)
        work = jnp.where(sel, -jnp.inf, work)
    exps = [jnp.exp(v - vals[0]) for v in vals]
    denom = exps[0]
    for e in exps[1:]:
        denom = denom + e
    gate_out = jnp.zeros((tm, TOP_K), _F32)
    for k in range(TOP_K):
        gate_out = jnp.where(k_iota == k, exps[k] / denom, gate_out)

    onehot = jnp.zeros((tm, n_exp), _F32)
    for sel in sels:
        onehot = onehot + sel.astype(_F32)
    r_iota = lax.broadcasted_iota(_I32, (tm, tm), 0)
    c_iota = lax.broadcasted_iota(_I32, (tm, tm), 1)
    lower = (c_iota < r_iota).astype(_BF16)
    pos = jnp.dot(lower, onehot.astype(_BF16), preferred_element_type=_F32) + carry
    rank_out = jnp.zeros((tm, TOP_K), _I32)
    for k in range(TOP_K):
        rk = jnp.sum(jnp.where(sels[k], pos, 0.0), axis=-1, keepdims=True).astype(_I32)
        rank_out = jnp.where(k_iota == k, rk, rank_out)
    return idx_out, gate_out, rank_out, carry + jnp.sum(onehot, axis=0, keepdims=True)


def _dispatch_body(pad_ref, dest_ref, hp_ref, xs_ref, zero_ref, sem):
    tm = hp_ref.shape[0]

    @pl.when(pl.program_id(0) == 0)
    def _():
        zero_ref[...] = jnp.zeros_like(zero_ref)
        n_zero = zero_ref.shape[0]
        copies = [pltpu.make_async_copy(
            zero_ref, xs_ref.at[pl.ds(pl.multiple_of(pad_ref[e], V7X_SUBLANES), n_zero)], sem)
            for e in range(pad_ref.shape[0])]
        for c in copies:
            c.start()
        for c in copies:
            c.wait()

    def row_copy(t, k):
        dst = xs_ref.at[pl.ds(dest_ref[t * TOP_K + k], 1)]
        return pltpu.make_async_copy(hp_ref.at[pl.ds(t, 1)], dst, sem)

    def start(t, c):
        for k in range(TOP_K):
            row_copy(t, k).start()
        return c

    def wait(t, c):
        for k in range(TOP_K):
            row_copy(t, k).wait()
        return c

    lax.fori_loop(0, tm, start, 0)
    lax.fori_loop(0, tm, wait, 0, unroll=DMA_WAIT_UNROLL)


def _dispatch(hp, dest_flat, seg_pad, n_slots):
    n, w = hp.shape
    tm = _tile(n, 256)
    return pl.pallas_call(
        _dispatch_body,
        grid_spec=pltpu.PrefetchScalarGridSpec(
            num_scalar_prefetch=1,
            grid=(n // tm,),
            in_specs=[
                pl.BlockSpec((tm * TOP_K,), lambda i, pd: (i,), memory_space=pltpu.SMEM),
                pl.BlockSpec((tm, w), lambda i, pd: (i, 0)),
            ],
            out_specs=pl.BlockSpec(memory_space=pl.ANY),
            scratch_shapes=[pltpu.VMEM((MOE_ZERO_ROWS, w), hp.dtype), pltpu.SemaphoreType.DMA(())],
        ),
        out_shape=jax.ShapeDtypeStruct((n_slots + MOE_ZERO_ROWS, w), hp.dtype),
        compiler_params=_cparams(("arbitrary",), 4 * _nbytes((tm, w), hp.dtype) + (4 << 20)),
        name="moe_dispatch",
    )(seg_pad, dest_flat, hp)


def _for_each_piece(rows_p, chunk, fn):
    base = rows_p // chunk * chunk
    rem = rows_p - base
    size = chunk // 2
    while size >= MOE_TAIL:
        def piece(size=size):
            off = base + (rem - rem % (2 * size))
            fn(pl.multiple_of(off, size), size)
        pl.when((rem // size) % 2 == 1)(piece)
        size //= 2


def _for_each_chunk(rows, chunk, fn, per_trip=1):
    rows_p = ((rows + (MOE_TAIL - 1)) // MOE_TAIL) * MOE_TAIL
    span = chunk * per_trip

    def full(c, carry):
        for i in range(per_trip):
            fn(pl.multiple_of(c * span + i * chunk, chunk), chunk)
        return carry

    lax.fori_loop(0, rows_p // span, full, 0)
    _for_each_piece(rows_p, span, fn)


def _unpack_halves(words):
    lo_cols = lax.bitcast_convert_type(words & jnp.uint32(0xFFFF0000), _F32)
    hi_cols = lax.bitcast_convert_type(words << 16, _F32)
    return lo_cols, hi_cols


def _gmm_gate_up_body(sbe_ref, blk_ref, rows_ref, nlive_ref, x_ref, wg_ref, wu_ref, bg_ref, bu_ref, o_ref):
    del sbe_ref, blk_ref, nlive_ref
    s = pl.program_id(0)
    half = x_ref.shape[1]

    def chunk(r0, size):
        xa, xb = _unpack_halves(x_ref[pl.ds(r0, size), :])
        g = (jnp.dot(xa, wg_ref[:half, :], preferred_element_type=_F32)
             + jnp.dot(xb, wg_ref[half:, :], preferred_element_type=_F32) + bg_ref[...])
        u = (jnp.dot(xa, wu_ref[:half, :], preferred_element_type=_F32)
             + jnp.dot(xb, wu_ref[half:, :], preferred_element_type=_F32) + bu_ref[...])
        g = jnp.minimum(g, SWIGLU_LIMIT)
        u = jnp.clip(u, -SWIGLU_LIMIT, SWIGLU_LIMIT)
        act = (u + 1.0) * (g * _sigmoid(g * SWIGLU_ALPHA))
        o_ref[pl.ds(r0, size), :] = act.astype(o_ref.dtype)

    _for_each_chunk(rows_ref[s], MOE_CHUNK_GATE_UP, chunk, per_trip=2)


def _gmm_down_body(sbe_ref, blk_ref, rows_ref, nlive_ref, a_ref, w_ref, b_ref, o_ref):
    del sbe_ref, blk_ref, nlive_ref
    s = pl.program_id(0)

    def chunk(r0, size):
        y = jnp.dot(a_ref[pl.ds(r0, size), :], w_ref[...], preferred_element_type=_F32) + b_ref[...]
        o_ref[pl.ds(r0, size), :] = y

    _for_each_chunk(rows_ref[s], MOE_CHUNK_DOWN, chunk)


def _expert_mlp(xs, meta, w_gu, b_gu, w_down, b_down, layer, cap):
    sbe, blk, sb_rows, n_live = meta
    n_sb = sbe.shape[0]
    n_slots, half = xs.shape
    d = 2 * half
    n_exp = w_gu.shape[1]
    d_ff = w_gu.shape[3] // 2
    tf = _tile(d_ff, 256)
    nf = d_ff // tf
    b_gu4 = b_gu.reshape(b_gu.shape[0], n_exp, 1, 2 * d_ff)

    def col(s, j, nl, n_tiles):
        return jnp.where(s < nl[0], j, n_tiles - 1)

    vmem1 = (_nbytes((cap, half), _U32) + 4 * _nbytes((d, tf), _F32) + 2 * _nbytes((cap, tf), _BF16)
             + _nbytes((MOE_CHUNK_GATE_UP, d), _F32) + 6 * _nbytes((MOE_CHUNK_GATE_UP, tf), _F32))
    act = pl.pallas_call(
        _gmm_gate_up_body,
        grid_spec=pltpu.PrefetchScalarGridSpec(
            num_scalar_prefetch=4,
            grid=(n_sb, nf),
            in_specs=[
                pl.BlockSpec((cap, half), lambda s, j, e, b, r, nl: (b[s], 0),
                             pipeline_mode=pl.Buffered(1)),
                pl.BlockSpec((None, None, d, tf), lambda s, j, e, b, r, nl: (layer, e[s], 0, col(s, j, nl, nf))),
                pl.BlockSpec((None, None, d, tf), lambda s, j, e, b, r, nl: (layer, e[s], 0, nf + col(s, j, nl, nf))),
                pl.BlockSpec((None, None, 1, tf), lambda s, j, e, b, r, nl: (layer, e[s], 0, col(s, j, nl, nf))),
                pl.BlockSpec((None, None, 1, tf), lambda s, j, e, b, r, nl: (layer, e[s], 0, nf + col(s, j, nl, nf))),
            ],
            out_specs=pl.BlockSpec((cap, tf), lambda s, j, e, b, r, nl: (b[s], col(s, j, nl, nf))),
        ),
        out_shape=jax.ShapeDtypeStruct((n_slots, d_ff), _BF16),
        compiler_params=_cparams(("arbitrary", "arbitrary"), vmem1 + (4 << 20)),
        name="moe_gate_up",
    )(sbe, blk, sb_rows, n_live, xs, w_gu, w_gu, b_gu4, b_gu4)

    tn = _tile(d, 512)
    nn = d // tn
    vmem2 = (_nbytes((cap, d_ff), _BF16) + 2 * _nbytes((d_ff, tn), _F32) + 2 * _nbytes((cap, tn), _F32)
             + 3 * _nbytes((MOE_CHUNK_DOWN, tn), _F32))
    ys = pl.pallas_call(
        _gmm_down_body,
        grid_spec=pltpu.PrefetchScalarGridSpec(
            num_scalar_prefetch=4,
            grid=(n_sb, nn),
            in_specs=[
                pl.BlockSpec((cap, d_ff), lambda s, j, e, b, r, nl: (b[s], 0),
                             pipeline_mode=pl.Buffered(1)),
                pl.BlockSpec((None, None, d_ff, tn), lambda s, j, e, b, r, nl: (layer, e[s], 0, col(s, j, nl, nn))),
                pl.BlockSpec((None, None, 1, tn), lambda s, j, e, b, r, nl: (layer, e[s], 0, col(s, j, nl, nn))),
            ],
            out_specs=pl.BlockSpec((cap, tn), lambda s, j, e, b, r, nl: (b[s], col(s, j, nl, nn))),
        ),
        out_shape=jax.ShapeDtypeStruct((n_slots, d), _F32),
        compiler_params=_cparams(("arbitrary", "arbitrary"), vmem2 + (4 << 20)),
        name="moe_down",
    )(sbe, blk, sb_rows, n_live, act, w_down, b_down.reshape(b_down.shape[0], n_exp, 1, d))
    return ys


def _combine_ln_body(dest_ref, dest_nxt_ref, gate_ref, h_ref, e_ref, g_ref, b_ref, ys_ref, o_ref, buf, sem,
                     *, alpha, n_steps):
    i = pl.program_id(0)
    tm = gate_ref.shape[0]
    slot = i % 2
    group = COMBINE_GROUP

    def row_copy(src_row, slot_, t, k):
        return pltpu.make_async_copy(ys_ref.at[pl.ds(src_row, 1)], buf.at[slot_, k, pl.ds(t, 1)],
                                     sem.at[slot_])

    def wait_slot(slot_):
        def wait(t, c):
            for k in range(TOP_K):
                row_copy(0, slot_, t, k).wait()
            return c
        lax.fori_loop(0, tm, wait, 0, unroll=DMA_WAIT_UNROLL)

    @pl.when(i == 0)
    def _():
        def first(t, c):
            for k in range(TOP_K):
                row_copy(dest_ref[t * TOP_K + k], 0, t, k).start()
            return c
        lax.fori_loop(0, tm, first, 0)

    wait_slot(slot)
    gamma, beta = g_ref[...], b_ref[...]

    def combine_group(gi, c):
        t0 = pl.multiple_of(gi * group, group)
        for j in range(group):
            for k in range(TOP_K):
                row_copy(dest_nxt_ref[(t0 + j) * TOP_K + k], 1 - slot, t0 + j, k).start()
        rows = pl.ds(t0, group)
        gate = gate_ref[rows, :]
        acc = alpha * h_ref[rows, :] + e_ref[rows, :]
        for k in range(TOP_K):
            acc = acc + gate[:, k:k + 1] * buf[slot, k, rows, :]
        o_ref[rows, :] = _layer_norm_rows(acc, gamma, beta)
        return c

    lax.fori_loop(0, tm // group, combine_group, 0)

    @pl.when(i == n_steps - 1)
    def _():
        wait_slot(1 - slot)


def _combine_ln(ys, dest_flat, gate, hf, e, g, b, alpha, row_off, rows):
    d = ys.shape[1]
    tm = _tile(rows, 128)
    n_steps = rows // tm
    off = row_off // tm
    assert row_off % tm == 0 and tm % COMBINE_GROUP == 0
    blk = _nbytes((tm, d), _F32)
    row = lambda i: (i + off, 0)
    return pl.pallas_call(
        functools.partial(_combine_ln_body, alpha=alpha, n_steps=n_steps),
        grid=(n_steps,),
        in_specs=[
            pl.BlockSpec((tm * TOP_K,), lambda i: (i + off,), memory_space=pltpu.SMEM),
            pl.BlockSpec((tm * TOP_K,), lambda i: (jnp.minimum(i + 1, n_steps - 1) + off,),
                         memory_space=pltpu.SMEM),
            pl.BlockSpec((tm, TOP_K), row),
            pl.BlockSpec((tm, d), row),
            pl.BlockSpec((tm, d), row),
            pl.BlockSpec((1, d), lambda i: (0, 0)),
            pl.BlockSpec((1, d), lambda i: (0, 0)),
            pl.BlockSpec(memory_space=pl.ANY),
        ],
        out_specs=pl.BlockSpec((tm, d), lambda i: (i, 0)),
        out_shape=jax.ShapeDtypeStruct((rows, d), _F32),
        scratch_shapes=[pltpu.VMEM((2, TOP_K, tm, d), _F32), pltpu.SemaphoreType.DMA((2,))],
        compiler_params=_cparams(("arbitrary",), (2 * TOP_K + 10) * blk + (4 << 20)),
        name="moe_combine_ln",
    )(dest_flat, dest_flat, gate, hf, e, g.reshape(1, d), b.reshape(1, d), ys)


def _moe_capacity(n_assign, n_exp):
    mean = -(-n_assign // n_exp)
    return -(-(mean + mean // 8) // MOE_TAIL) * MOE_TAIL


def _moe_plan(counts, cap, n_sb):
    n_exp = counts.shape[0]
    sb_per_e = (counts + cap - 1) // cap
    sb_end = jnp.cumsum(sb_per_e)
    sb_start = sb_end - sb_per_e
    n_live = sb_end[-1]
    sidx = jnp.arange(n_sb, dtype=_I32)
    e_of = jnp.minimum(jnp.sum(sidx[:, None] >= sb_end[None, :], axis=1), n_exp - 1).astype(_I32)
    of_e = e_of[:, None] == jnp.arange(n_exp, dtype=_I32)[None, :]
    local = sidx - jnp.sum(jnp.where(of_e, sb_start[None, :], 0), axis=1)
    cnt = jnp.sum(jnp.where(of_e, counts[None, :], 0), axis=1)
    live = sidx < n_live
    rows = jnp.where(live, jnp.clip(cnt - local * cap, 0, cap), 0).astype(_I32)
    last = jnp.maximum(n_live - 1, 0)
    e_last = jnp.sum(jnp.where(sidx == last, e_of, 0))
    sbe = jnp.where(live, e_of, e_last).astype(_I32)
    blk = jnp.where(live, sidx, last).astype(_I32)
    seg_start = (sb_start * cap).astype(_I32)
    meta = (sbe, blk, rows, n_live.reshape(1).astype(_I32))
    seg_pad = (seg_start + counts) // V7X_SUBLANES * V7X_SUBLANES
    return meta, seg_start, seg_pad.astype(_I32)


def _moe_experts(hp, top_idx, rank, counts, w_gu, b_gu, w_down, b_down, layer):
    n = hp.shape[0]
    n_exp = counts.shape[1]
    n_assign = n * TOP_K
    cap = _moe_capacity(n_assign, n_exp)
    n_sb = n_assign // cap + n_exp
    meta, seg_start, seg_pad = _moe_plan(counts[0], cap, n_sb)
    of_e = top_idx[:, :, None] == jnp.arange(n_exp, dtype=_I32)[None, None, :]
    dest = jnp.sum(jnp.where(of_e, seg_start[None, None, :], 0), axis=-1) + rank
    dest_flat = dest.reshape(-1).astype(_I32)
    xs = _dispatch(hp, dest_flat, seg_pad, n_sb * cap)
    ys = _expert_mlp(xs, meta, w_gu, b_gu, w_down, b_down, layer, cap)
    return ys, dest_flat


def kernel(x_prompt, x_sample, p_prompt, p_sample, ln_emb_g, ln_emb_b, w_in, b_in, rpb, sgu_ln_g, sgu_ln_b, sgu_w, sgu_b, w_pa, w_pb, w_o, ln1_g, ln1_b, w_router, b_router, w_gu, b_gu, w_down, b_down, w_pg, w_ple, ln2_g, ln2_b):
    depth = w_in.shape[0]
    d = x_prompt.shape[-1]
    ba, ta = x_prompt.shape[:2]
    bb, tb = x_sample.shape[:2]
    na_rows, nb_rows = ba * ta, bb * tb
    na_width = w_pa.shape[1]
    sgu_width = w_pb.shape[1]
    heads = na_width // NA_HEAD_DIM
    alpha = (2.0 * depth) ** 0.25
    qkv_cols = 3 * na_width

    xf, xb = _ln_emb(x_prompt.reshape(na_rows, d), x_sample.reshape(nb_rows, d), ln_emb_g, ln_emb_b)
    p_all = jnp.concatenate([p_prompt.reshape(depth, na_rows, -1), p_sample.reshape(depth, nb_rows, -1)], axis=1)

    for l in range(depth):
        zqkv = _in_proj(xb, w_in, b_in, l, 0, qkv_cols, _BF16)
        zrest = _in_proj(xb, w_in, b_in, l, qkv_cols, w_in.shape[2] - qkv_cols, _F32)
        bias_tab = _na_bias_table(rpb[l])
        na_out = jnp.concatenate([
            _neighbourhood_attention(zqkv, bias_tab, 0, ba, ta, heads),
            _neighbourhood_attention(zqkv, bias_tab, na_rows, bb, tb, heads)], axis=0)
        sgu_out = _spatial_gating(zrest, sgu_ln_g, sgu_ln_b, sgu_w, sgu_b, l, 0, sgu_width, sgu_width)
        merged = _merge(na_out, sgu_out, w_pa, w_pb, zrest, l, 2 * sgu_width, 2 * sgu_width + d)
        s1 = _out_proj(merged, w_o, xf, l, alpha)
        hf, hb, hp, top_idx, gate, rank, counts = _ln_route(s1, ln1_g[l], ln1_b[l], w_router, b_router, l)
        ys, dest_flat = _moe_experts(hp, top_idx, rank, counts, w_gu, b_gu, w_down, b_down, l)
        e = _ple_gate(hb, w_pg, p_all[l], w_ple, l)

        def layer_out(row_off, rows, l=l, ys=ys, dest_flat=dest_flat, gate=gate, hf=hf, e=e):
            return _combine_ln(ys, dest_flat, gate, hf, e, ln2_g[l], ln2_b[l], alpha, row_off, rows)

        if l + 1 < depth:
            xf = layer_out(0, na_rows + nb_rows)
            xb = xf.astype(_BF16)

    y_a = layer_out(0, na_rows)
    y_b = layer_out(na_rows, nb_rows)
    return (y_a.reshape(ba, ta, d), y_b.reshape(bb, tb, d))
```

```python
import functools

import jax
import jax.numpy as jnp
import numpy as np
from jax import lax
from jax.experimental import pallas as pl
from jax.experimental.pallas import tpu as pltpu

_F32 = jnp.float32
_BF16 = jnp.bfloat16
_I32 = jnp.int32
_U32 = jnp.uint32

LN_EPS = 1e-5
GRID_W = 64
WIN_R = 8
WIN_C = 16
NA_HEAD_DIM = 128
TOP_K = 4
SWIGLU_ALPHA = 1.702
SWIGLU_LIMIT = 7.0
MASK_VALUE = -1e30
NA_BLOCK_ROWS = 4
NA_KEY_ROWS = 12
NA_UNROLL = 4

V7X_VMEM_BYTES = 64 * 1024 * 1024
V7X_LANES = 128
V7X_SUBLANES = 8
V7X_MXU_DIM = 256
MOE_CHUNK_GATE_UP = 2 * V7X_MXU_DIM
MOE_CHUNK_DOWN = 4 * V7X_MXU_DIM
MOE_TAIL = V7X_MXU_DIM // 2
MOE_LOAD_ROWS = V7X_MXU_DIM
MOE_ZERO_ROWS = MOE_TAIL + V7X_SUBLANES
DMA_WAIT_UNROLL = 8
COMBINE_GROUP = 4 * V7X_SUBLANES


def _cparams(semantics, vmem_bytes):
    limit = min(int(vmem_bytes), V7X_VMEM_BYTES - 4 * 1024 * 1024)
    return pltpu.CompilerParams(dimension_semantics=semantics, vmem_limit_bytes=limit)


def _nbytes(shape, dtype):
    n = 1
    for s in shape:
        n *= s
    return n * jnp.dtype(dtype).itemsize


def _tile(dim, pref):
    t = min(dim, pref)
    while dim % t:
        t //= 2
    return t


def _layer_norm_rows(x, g, b):
    mu = jnp.mean(x, axis=-1, keepdims=True)
    xc = x - mu
    var = jnp.mean(xc * xc, axis=-1, keepdims=True)
    return xc * lax.rsqrt(var + LN_EPS) * g + b


def _gelu(x):
    return 0.5 * x * (1.0 + lax.erf(x * 0.7071067811865476))


def _sigmoid(x):
    return 1.0 / (1.0 + jnp.exp(-x))


def _ln_emb_body(xa_ref, xb_ref, g_ref, b_ref, of_ref, ob_ref, *, na):
    i = pl.program_id(0)

    def emit(x_ref):
        y = _layer_norm_rows(x_ref[...], g_ref[...], b_ref[...])
        of_ref[...] = y
        ob_ref[...] = y.astype(_BF16)

    @pl.when(i < na)
    def _():
        emit(xa_ref)

    @pl.when(i >= na)
    def _():
        emit(xb_ref)


def _ln_emb(xa, xb, g, b):
    d = xa.shape[1]
    tm = _tile(min(xa.shape[0], xb.shape[0]), 256)
    na, nb = xa.shape[0] // tm, xb.shape[0] // tm
    n = xa.shape[0] + xb.shape[0]
    blk = _nbytes((tm, d), _F32)
    return pl.pallas_call(
        functools.partial(_ln_emb_body, na=na),
        grid=(na + nb,),
        in_specs=[
            pl.BlockSpec((tm, d), lambda i: (jnp.minimum(i, na - 1), 0)),
            pl.BlockSpec((tm, d), lambda i: (jnp.maximum(i - na, 0), 0)),
            pl.BlockSpec((1, d), lambda i: (0, 0)),
            pl.BlockSpec((1, d), lambda i: (0, 0)),
        ],
        out_specs=[pl.BlockSpec((tm, d), lambda i: (i, 0)),
                   pl.BlockSpec((tm, d), lambda i: (i, 0))],
        out_shape=[jax.ShapeDtypeStruct((n, d), _F32), jax.ShapeDtypeStruct((n, d), _BF16)],
        compiler_params=_cparams(("arbitrary",), 12 * blk),
        name="ln_emb",
    )(xa, xb, g.reshape(1, d), b.reshape(1, d))


def _ln_route_body(s_ref, g_ref, b_ref, wr_hi_ref, wr_lo_ref, br_ref, hf_ref, hb_ref, hp_ref, idx_ref,
                   gate_ref, rank_ref, cnt_ref, carry_ref):
    y = _layer_norm_rows(s_ref[...], g_ref[...], b_ref[...])
    hf_ref[...] = y
    yb = y.astype(_BF16)
    hb_ref[...] = yb
    bits = lax.bitcast_convert_type(yb.astype(_F32), _U32)
    half = bits.shape[1] // 2
    hp_ref[...] = bits[:, :half] | (bits[:, half:] >> 16)

    @pl.when(pl.program_id(0) == 0)
    def _():
        carry_ref[...] = jnp.zeros_like(carry_ref)

    y_lo = (y - yb.astype(_F32)).astype(_BF16)
    logits = (jnp.dot(yb, wr_hi_ref[...], preferred_element_type=_F32)
              + jnp.dot(yb, wr_lo_ref[...], preferred_element_type=_F32)
              + jnp.dot(y_lo, wr_hi_ref[...], preferred_element_type=_F32) + br_ref[...])
    idx, gate, rank, carry = _route(logits, carry_ref[...])
    carry_ref[...] = carry
    idx_ref[...] = idx
    gate_ref[...] = gate
    rank_ref[...] = rank
    cnt_ref[...] = carry.astype(_I32)


def _ln_route(s, g, b, w_router, b_router, layer):
    n, d = s.shape
    n_exp = w_router.shape[2]
    tm = _tile(n, 256)
    blk = _nbytes((tm, d), _F32)
    row = lambda i: (i, 0)
    w_hi = w_router[layer].astype(_BF16)
    w_lo = (w_router[layer] - w_hi.astype(_F32)).astype(_BF16)
    return pl.pallas_call(
        _ln_route_body,
        grid=(n // tm,),
        in_specs=[pl.BlockSpec((tm, d), row),
                  pl.BlockSpec((1, d), lambda i: (0, 0)),
                  pl.BlockSpec((1, d), lambda i: (0, 0)),
                  pl.BlockSpec((d, n_exp), lambda i: (0, 0)),
                  pl.BlockSpec((d, n_exp), lambda i: (0, 0)),
                  pl.BlockSpec((None, 1, n_exp), lambda i: (layer, 0, 0))],
        out_specs=[pl.BlockSpec((tm, d), row),
                   pl.BlockSpec((tm, d), row),
                   pl.BlockSpec((tm, d // 2), row),
                   pl.BlockSpec((tm, TOP_K), row),
                   pl.BlockSpec((tm, TOP_K), row),
                   pl.BlockSpec((tm, TOP_K), row),
                   pl.BlockSpec((1, n_exp), lambda i: (0, 0))],
        out_shape=[jax.ShapeDtypeStruct((n, d), _F32),
                   jax.ShapeDtypeStruct((n, d), _BF16),
                   jax.ShapeDtypeStruct((n, d // 2), _U32),
                   jax.ShapeDtypeStruct((n, TOP_K), _I32),
                   jax.ShapeDtypeStruct((n, TOP_K), _F32),
                   jax.ShapeDtypeStruct((n, TOP_K), _I32),
                   jax.ShapeDtypeStruct((1, n_exp), _I32)],
        scratch_shapes=[pltpu.VMEM((1, n_exp), _F32)],
        compiler_params=_cparams(("arbitrary",), 14 * blk + 2 * _nbytes((d, V7X_LANES), _F32)),
        name="ln_route",
    )(s, g.reshape(1, d), b.reshape(1, d), w_hi, w_lo, b_router.reshape(b_router.shape[0], 1, n_exp))


def _in_proj_body(a_ref, w_ref, b_ref, o_ref):
    acc = jnp.dot(a_ref[...], w_ref[...], preferred_element_type=_F32)
    o_ref[...] = (acc + b_ref[...]).astype(o_ref.dtype)


def _in_proj(a, w, bias, layer, col_off, cols, out_dtype):
    m, k = a.shape
    bm = _tile(m, 2048)
    bn = _tile(cols, 512)
    joff = col_off // bn
    bias3 = bias.reshape(bias.shape[0], 1, bias.shape[1])
    vmem = (_nbytes((bm, k), a.dtype) + 2 * _nbytes((k, bn), w.dtype)
            + 2 * _nbytes((bm, bn), out_dtype) + 2 * _nbytes((bm, bn), _F32))
    return pl.pallas_call(
        _in_proj_body,
        grid=(m // bm, cols // bn),
        in_specs=[
            pl.BlockSpec((bm, k), lambda i, j: (i, 0), pipeline_mode=pl.Buffered(1)),
            pl.BlockSpec((None, k, bn), lambda i, j: (layer, 0, j + joff)),
            pl.BlockSpec((None, 1, bn), lambda i, j: (layer, 0, j + joff)),
        ],
        out_specs=pl.BlockSpec((bm, bn), lambda i, j: (i, j)),
        out_shape=jax.ShapeDtypeStruct((m, cols), out_dtype),
        compiler_params=_cparams(("arbitrary", "arbitrary"), vmem + (4 << 20)),
        name="in_proj",
    )(a, w, bias3)


def _merge_body(na_ref, sg_ref, wa_ref, wb_ref, ga_ref, gb_ref, o_ref):
    ya = jnp.dot(na_ref[...], wa_ref[...], preferred_element_type=_F32)
    yb = jnp.dot(sg_ref[...], wb_ref[...], preferred_element_type=_F32)
    merged = _sigmoid(ga_ref[...]) * ya + _sigmoid(gb_ref[...]) * yb
    o_ref[...] = merged.astype(o_ref.dtype)


def _merge(na_out, sgu_out, w_pa, w_pb, zrest, layer, ga_col, gb_col):
    m, ka = na_out.shape
    kb = sgu_out.shape[1]
    d = w_pa.shape[2]
    bm = _tile(m, 1024)
    bn = _tile(d, 512)
    ja, jb = ga_col // bn, gb_col // bn
    vmem = 2 * (_nbytes((bm, ka), _BF16) + _nbytes((bm, kb), _BF16) + _nbytes((ka, bn), _F32)
                + _nbytes((kb, bn), _F32) + 2 * _nbytes((bm, bn), _F32) + _nbytes((bm, bn), _BF16))
    vmem += 3 * _nbytes((bm, bn), _F32)
    return pl.pallas_call(
        _merge_body,
        grid=(m // bm, d // bn),
        in_specs=[
            pl.BlockSpec((bm, ka), lambda i, j: (i, 0)),
            pl.BlockSpec((bm, kb), lambda i, j: (i, 0)),
            pl.BlockSpec((None, ka, bn), lambda i, j: (layer, 0, j)),
            pl.BlockSpec((None, kb, bn), lambda i, j: (layer, 0, j)),
            pl.BlockSpec((bm, bn), lambda i, j: (i, j + ja)),
            pl.BlockSpec((bm, bn), lambda i, j: (i, j + jb)),
        ],
        out_specs=pl.BlockSpec((bm, bn), lambda i, j: (i, j)),
        out_shape=jax.ShapeDtypeStruct((m, d), _BF16),
        compiler_params=_cparams(("arbitrary", "arbitrary"), vmem + (4 << 20)),
        name="merge",
    )(na_out, sgu_out, w_pa, w_pb, zrest, zrest)


def _out_proj_body(a_ref, w_ref, x_ref, o_ref, *, alpha):
    acc = jnp.dot(a_ref[...], w_ref[...], preferred_element_type=_F32)
    o_ref[...] = alpha * x_ref[...] + acc


def _out_proj(a, w, resid, layer, alpha):
    m, k = a.shape
    d = w.shape[2]
    bm = _tile(m, 1024)
    bn = _tile(d, 512)
    vmem = 2 * (_nbytes((bm, k), a.dtype) + _nbytes((k, bn), w.dtype) + 2 * _nbytes((bm, bn), _F32))
    vmem += 2 * _nbytes((bm, bn), _F32)
    return pl.pallas_call(
        functools.partial(_out_proj_body, alpha=alpha),
        grid=(m // bm, d // bn),
        in_specs=[
            pl.BlockSpec((bm, k), lambda i, j: (i, 0)),
            pl.BlockSpec((None, k, bn), lambda i, j: (layer, 0, j)),
            pl.BlockSpec((bm, bn), lambda i, j: (i, j)),
        ],
        out_specs=pl.BlockSpec((bm, bn), lambda i, j: (i, j)),
        out_shape=jax.ShapeDtypeStruct((m, d), _F32),
        compiler_params=_cparams(("arbitrary", "arbitrary"), vmem + (4 << 20)),
        name="out_proj",
    )(a, w, resid)


def _ple_gate_body(a_ref, w_ref, p_ref, wp_ref, o_ref):
    gate = _sigmoid(jnp.dot(a_ref[...], w_ref[...], preferred_element_type=_F32))
    ple = jnp.dot(p_ref[...], wp_ref[...], preferred_element_type=_F32)
    o_ref[...] = gate * ple


def _ple_gate(hb, w_pg, p, w_ple, layer):
    m, k = hb.shape
    d = w_pg.shape[2]
    kp = p.shape[1]
    bm = _tile(m, 2048)
    bn = _tile(d, 512)
    vmem = (_nbytes((bm, k), hb.dtype) + _nbytes((bm, kp), _F32)
            + 2 * (_nbytes((k, bn), w_pg.dtype) + _nbytes((kp, bn), _F32) + _nbytes((bm, bn), _F32))
            + 3 * _nbytes((bm, bn), _F32))
    return pl.pallas_call(
        _ple_gate_body,
        grid=(m // bm, d // bn),
        in_specs=[
            pl.BlockSpec((bm, k), lambda i, j: (i, 0), pipeline_mode=pl.Buffered(1)),
            pl.BlockSpec((None, k, bn), lambda i, j: (layer, 0, j)),
            pl.BlockSpec((bm, kp), lambda i, j: (i, 0), pipeline_mode=pl.Buffered(1)),
            pl.BlockSpec((None, kp, bn), lambda i, j: (layer, 0, j)),
        ],
        out_specs=pl.BlockSpec((bm, bn), lambda i, j: (i, j)),
        out_shape=jax.ShapeDtypeStruct((m, d), _F32),
        compiler_params=_cparams(("arbitrary", "arbitrary"), vmem + (4 << 20)),
        name="ple_gate",
    )(hb, w_pg, p, w_ple)


def _na_bias_table(rpb_l):
    heads = rpb_l.shape[0]
    half = WIN_R // 2
    i = np.arange(NA_BLOCK_ROWS)
    win_off = np.stack([np.zeros_like(i), i, np.full_like(i, NA_KEY_ROWS - WIN_R)])
    win_d = np.stack([i, np.full_like(i, half), half + i])
    kr = np.arange(NA_KEY_ROWS)
    j = kr[None, None, :] - win_off[:, :, None]
    row_ok = (j >= 0) & (j < WIN_R)
    dr = j - win_d[:, :, None] + (WIN_R - 1)
    n_dr, n_dc = 2 * WIN_R - 1, 2 * WIN_C - 1
    row_sel = np.where(row_ok, dr, n_dr)[..., None] == np.arange(n_dr + 1)
    cols = np.arange(GRID_W)
    col_start = np.clip(cols - WIN_C // 2, 0, GRID_W - WIN_C)
    col_ok = (cols[None, :] >= col_start[:, None]) & (cols[None, :] < col_start[:, None] + WIN_C)
    dc = cols[None, :] - cols[:, None] + (WIN_C - 1)
    col_sel = np.where(col_ok, dc, n_dc)[..., None] == np.arange(n_dc + 1)
    values = jnp.pad(rpb_l.astype(_F32), ((0, 0), (0, 1), (0, 1)), constant_values=MASK_VALUE)
    tab = jnp.einsum("hab,pika,qcb->hpiqkc", values, row_sel.astype(np.float32),
                     col_sel.astype(np.float32), precision=lax.Precision.HIGHEST)
    return tab.reshape(heads, 3, NA_BLOCK_ROWS * GRID_W, NA_KEY_ROWS * GRID_W)


def _na_body(q_ref, k_ref, v_ref, bias_ref, o_ref, *, rows, scale):
    n_q = NA_BLOCK_ROWS * GRID_W
    n_k = NA_KEY_ROWS * GRID_W
    n_blocks = rows // NA_BLOCK_ROWS

    def block_fn(ib, carry):
        r0 = ib * NA_BLOCK_ROWS
        kb0 = jnp.clip(r0 - WIN_R // 2, 0, rows - NA_KEY_ROWS)
        pattern = jnp.where(ib == 0, 0, jnp.where(ib == n_blocks - 1, 2, 1))
        q0 = pl.multiple_of(r0 * GRID_W, n_q)
        k0 = pl.multiple_of(kb0 * GRID_W, NA_BLOCK_ROWS * GRID_W)
        q = q_ref[pl.ds(q0, n_q), :]
        kw = k_ref[pl.ds(k0, n_k), :]
        vw = v_ref[pl.ds(k0, n_k), :]
        s = lax.dot_general(q, kw, (((1,), (1,)), ((), ())), preferred_element_type=_F32)
        s = s * scale + bias_ref[pattern]
        m = jnp.max(s, axis=-1, keepdims=True)
        e = jnp.exp(s - m)
        p = e / jnp.sum(e, axis=-1, keepdims=True)
        o = jnp.dot(p.astype(_BF16), vw, preferred_element_type=_F32)
        o_ref[pl.ds(q0, n_q), :] = o.astype(o_ref.dtype)
        return carry

    lax.fori_loop(0, n_blocks, block_fn, 0, unroll=NA_UNROLL)


def _neighbourhood_attention(zqkv, bias_tab, row_off, batch, seq, heads):
    assert seq % GRID_W == 0 and row_off % seq == 0
    rows = seq // GRID_W
    assert rows % NA_BLOCK_ROWS == 0 and rows >= NA_KEY_ROWS
    boff = row_off // seq
    dh = NA_HEAD_DIM
    n_q = NA_BLOCK_ROWS * GRID_W
    n_k = NA_KEY_ROWS * GRID_W
    vmem = 2 * (4 * _nbytes((seq, dh), _BF16) + _nbytes((3, n_q, n_k), _F32)) + 6 * _nbytes((n_q, n_k), _F32)
    return pl.pallas_call(
        functools.partial(_na_body, rows=rows, scale=dh ** -0.5),
        grid=(heads, batch),
        in_specs=[
            pl.BlockSpec((seq, dh), lambda h, b: (b + boff, h)),
            pl.BlockSpec((seq, dh), lambda h, b: (b + boff, heads + h)),
            pl.BlockSpec((seq, dh), lambda h, b: (b + boff, 2 * heads + h)),
            pl.BlockSpec((None, 3, n_q, n_k), lambda h, b: (h, 0, 0, 0)),
        ],
        out_specs=pl.BlockSpec((seq, dh), lambda h, b: (b, h)),
        out_shape=jax.ShapeDtypeStruct((batch * seq, heads * dh), _BF16),
        compiler_params=_cparams(("arbitrary", "arbitrary"), vmem),
        name="neighbourhood_attention",
    )(zqkv, zqkv, zqkv, bias_tab)


def _sgu_body(u_ref, v_ref, g_ref, b_ref, ws_ref, bs_ref, o_ref, *, groups):
    u = _gelu(u_ref[...])
    vn = _layer_norm_rows(_gelu(v_ref[...]), g_ref[...], b_ref[...])
    gw = u.shape[1] // groups
    for g in range(groups):
        sl = slice(g * gw, (g + 1) * gw)
        s = jnp.dot(ws_ref[g], vn[:, sl], preferred_element_type=_F32) + bs_ref[g]
        o_ref[:, sl] = (u[:, sl] * s).astype(o_ref.dtype)


def _spatial_gating(zrest, ln_g, ln_b, w_s, b_s, layer, u_col, v_col, width):
    n = zrest.shape[0]
    groups, chunk = w_s.shape[1], w_s.shape[2]
    ju, jv = u_col // width, v_col // width
    blk = _nbytes((chunk, width), _F32)
    return pl.pallas_call(
        functools.partial(_sgu_body, groups=groups),
        grid=(n // chunk,),
        in_specs=[
            pl.BlockSpec((chunk, width), lambda i: (i, ju)),
            pl.BlockSpec((chunk, width), lambda i: (i, jv)),
            pl.BlockSpec((None, 1, width), lambda i: (layer, 0, 0)),
            pl.BlockSpec((None, 1, width), lambda i: (layer, 0, 0)),
            pl.BlockSpec((None, groups, chunk, chunk), lambda i: (layer, 0, 0, 0)),
            pl.BlockSpec((None, groups, chunk, 1), lambda i: (layer, 0, 0, 0)),
        ],
        out_specs=pl.BlockSpec((chunk, width), lambda i: (i, 0)),
        out_shape=jax.ShapeDtypeStruct((n, width), _BF16),
        compiler_params=_cparams(("arbitrary",), 12 * blk + (4 << 20)),
        name="spatial_gating",
    )(zrest, zrest, ln_g.reshape(-1, 1, width), ln_b.reshape(-1, 1, width), w_s,
      b_s.reshape(b_s.shape[0], groups, chunk, 1))


def _route(logits, carry):
    tm, n_exp = logits.shape
    e_iota = lax.broadcasted_iota(_I32, (tm, n_exp), 1)
    k_iota = lax.broadcasted_iota(_I32, (tm, TOP_K), 1)
    work = logits
    sels, vals = [], []
    idx_out = jnp.zeros((tm, TOP_K), _I32)
    for k in range(TOP_K):
        m = jnp.max(work, axis=-1, keepdims=True)
        idx = jnp.min(jnp.where(work == m, e_iota, n_exp), axis=-1, keepdims=True)
        sel = e_iota == idx
        sels.append(sel)
        vals.append(m)
        idx_out = jnp.where(k_iota == k, idx, idx_out)
        work = jnp.where(sel, -jnp.inf, work)
    exps = [jnp.exp(v - vals[0]) for v in vals]
    denom = exps[0]
    for e in exps[1:]:
        denom = denom + e
    gate_out = jnp.zeros((tm, TOP_K), _F32)
    for k in range(TOP_K):
        gate_out = jnp.where(k_iota == k, exps[k] / denom, gate_out)

    onehot = jnp.zeros((tm, n_exp), _F32)
    for sel in sels:
        onehot = onehot + sel.astype(_F32)
    r_iota = lax.broadcasted_iota(_I32, (tm, tm), 0)
    c_iota = lax.broadcasted_iota(_I32, (tm, tm), 1)
    lower = (c_iota < r_iota).astype(_BF16)
    pos = jnp.dot(lower, onehot.astype(_BF16), preferred_element_type=_F32) + carry
    rank_out = jnp.zeros((tm, TOP_K), _I32)
    for k in range(TOP_K):
        rk = jnp.sum(jnp.where(sels[k], pos, 0.0), axis=-1, keepdims=True).astype(_I32)
        rank_out = jnp.where(k_iota == k, rk, rank_out)
    return idx_out, gate_out, rank_out, carry + jnp.sum(onehot, axis=0, keepdims=True)


def _dispatch_body(pad_ref, dest_ref, hp_ref, xs_ref, zero_ref, sem):
    tm = hp_ref.shape[0]

    @pl.when(pl.program_id(0) == 0)
    def _():
        zero_ref[...] = jnp.zeros_like(zero_ref)
        n_zero = zero_ref.shape[0]
        copies = [pltpu.make_async_copy(
            zero_ref, xs_ref.at[pl.ds(pl.multiple_of(pad_ref[e], V7X_SUBLANES), n_zero)], sem)
            for e in range(pad_ref.shape[0])]
        for c in copies:
            c.start()
        for c in copies:
            c.wait()

    def row_copy(t, k):
        dst = xs_ref.at[pl.ds(dest_ref[t * TOP_K + k], 1)]
        return pltpu.make_async_copy(hp_ref.at[pl.ds(t, 1)], dst, sem)

    def start(t, c):
        for k in range(TOP_K):
            row_copy(t, k).start()
        return c

    def wait(t, c):
        for k in range(TOP_K):
            row_copy(t, k).wait()
        return c

    lax.fori_loop(0, tm, start, 0)
    lax.fori_loop(0, tm, wait, 0, unroll=DMA_WAIT_UNROLL)


def _dispatch(hp, dest_flat, seg_pad, n_slots):
    n, w = hp.shape
    tm = _tile(n, 256)
    return pl.pallas_call(
        _dispatch_body,
        grid_spec=pltpu.PrefetchScalarGridSpec(
            num_scalar_prefetch=1,
            grid=(n // tm,),
            in_specs=[
                pl.BlockSpec((tm * TOP_K,), lambda i, pd: (i,), memory_space=pltpu.SMEM),
                pl.BlockSpec((tm, w), lambda i, pd: (i, 0)),
            ],
            out_specs=pl.BlockSpec(memory_space=pl.ANY),
            scratch_shapes=[pltpu.VMEM((MOE_ZERO_ROWS, w), hp.dtype), pltpu.SemaphoreType.DMA(())],
        ),
        out_shape=jax.ShapeDtypeStruct((n_slots + MOE_ZERO_ROWS, w), hp.dtype),
        compiler_params=_cparams(("arbitrary",), 4 * _nbytes((tm, w), hp.dtype) + (4 << 20)),
        name="moe_dispatch",
    )(seg_pad, dest_flat, hp)


def _for_each_piece(rows_p, chunk, fn):
    base = rows_p // chunk * chunk
    rem = rows_p - base
    size = chunk // 2
    while size >= MOE_TAIL:
        def piece(size=size):
            off = base + (rem - rem % (2 * size))
            fn(pl.multiple_of(off, size), size)
        pl.when((rem // size) % 2 == 1)(piece)
        size //= 2


def _for_each_chunk(rows, chunk, fn, ready, per_trip=1):
    rows_p = ((rows + (MOE_TAIL - 1)) // MOE_TAIL) * MOE_TAIL
    span = chunk * per_trip

    def full(c, carry):
        ready(pl.multiple_of(c * span, span), span)
        for i in range(per_trip):
            fn(pl.multiple_of(c * span + i * chunk, chunk), chunk)
        return carry

    def piece(r0, size):
        ready(r0, size)
        fn(r0, size)

    lax.fori_loop(0, rows_p // span, full, 0)
    _for_each_piece(rows_p, span, piece)


def _unpack_halves(words):
    lo_cols = lax.bitcast_convert_type(words & jnp.uint32(0xFFFF0000), _F32)
    hi_cols = lax.bitcast_convert_type(words << 16, _F32)
    return lo_cols, hi_cols


def _superblock_loader(src_ref, buf_ref, sem, row0, rows_p):
    n_pieces = buf_ref.shape[0] // MOE_LOAD_ROWS

    def piece(c):
        src = src_ref.at[pl.ds(pl.multiple_of(row0 + c * MOE_LOAD_ROWS, MOE_LOAD_ROWS), MOE_LOAD_ROWS)]
        dst = buf_ref.at[pl.ds(pl.multiple_of(c * MOE_LOAD_ROWS, MOE_LOAD_ROWS), MOE_LOAD_ROWS)]
        return pltpu.make_async_copy(src, dst, sem.at[c])

    def start_all():
        for c in range(n_pieces):
            pl.when(c * MOE_LOAD_ROWS < rows_p)(lambda c=c: piece(c).start())

    def wait_rows(r0, size):
        for i in range(max(size // MOE_LOAD_ROWS, 1)):
            piece(r0 // MOE_LOAD_ROWS + i).wait()

    return start_all, wait_rows


def _gmm_gate_up_body(sbe_ref, blk_ref, rows_ref, nlive_ref, xs_ref, wg_ref, wu_ref, bg_ref, bu_ref, o_ref,
                      x_ref, sem):
    del sbe_ref, nlive_ref
    s = pl.program_id(0)
    first_tile = pl.program_id(1) == 0
    half = x_ref.shape[1]
    rows_p = ((rows_ref[s] + (MOE_TAIL - 1)) // MOE_TAIL) * MOE_TAIL
    start_all, wait_rows = _superblock_loader(xs_ref, x_ref, sem, blk_ref[s] * x_ref.shape[0], rows_p)
    pl.when(first_tile)(start_all)

    def ready(r0, size):
        pl.when(first_tile)(lambda: wait_rows(r0, size))

    def chunk(r0, size):
        xa, xb = _unpack_halves(x_ref[pl.ds(r0, size), :])
        g = (jnp.dot(xa, wg_ref[:half, :], preferred_element_type=_F32)
             + jnp.dot(xb, wg_ref[half:, :], preferred_element_type=_F32) + bg_ref[...])
        u = (jnp.dot(xa, wu_ref[:half, :], preferred_element_type=_F32)
             + jnp.dot(xb, wu_ref[half:, :], preferred_element_type=_F32) + bu_ref[...])
        g = jnp.minimum(g, SWIGLU_LIMIT)
        u = jnp.clip(u, -SWIGLU_LIMIT, SWIGLU_LIMIT)
        act = (u + 1.0) * (g * _sigmoid(g * SWIGLU_ALPHA))
        o_ref[pl.ds(r0, size), :] = act.astype(o_ref.dtype)

    _for_each_chunk(rows_ref[s], MOE_CHUNK_GATE_UP, chunk, ready, per_trip=2)


def _gmm_down_body(sbe_ref, blk_ref, rows_ref, nlive_ref, act_ref, w_ref, b_ref, o_ref, a_ref, sem):
    del sbe_ref, nlive_ref
    s = pl.program_id(0)
    first_tile = pl.program_id(1) == 0
    rows_p = ((rows_ref[s] + (MOE_TAIL - 1)) // MOE_TAIL) * MOE_TAIL
    start_all, wait_rows = _superblock_loader(act_ref, a_ref, sem, blk_ref[s] * a_ref.shape[0], rows_p)
    pl.when(first_tile)(start_all)

    def ready(r0, size):
        pl.when(first_tile)(lambda: wait_rows(r0, size))

    def chunk(r0, size):
        y = jnp.dot(a_ref[pl.ds(r0, size), :], w_ref[...], preferred_element_type=_F32) + b_ref[...]
        o_ref[pl.ds(r0, size), :] = y

    _for_each_chunk(rows_ref[s], MOE_CHUNK_DOWN, chunk, ready)


def _expert_mlp(xs, meta, w_gu, b_gu, w_down, b_down, layer, cap):
    sbe, blk, sb_rows, n_live = meta
    n_sb = sbe.shape[0]
    n_slots, half = xs.shape
    d = 2 * half
    n_exp = w_gu.shape[1]
    d_ff = w_gu.shape[3] // 2
    tf = _tile(d_ff, 256)
    nf = d_ff // tf
    b_gu4 = b_gu.reshape(b_gu.shape[0], n_exp, 1, 2 * d_ff)

    def col(s, j, nl, n_tiles):
        return jnp.where(s < nl[0], j, n_tiles - 1)

    vmem1 = (_nbytes((cap, half), _U32) + 4 * _nbytes((d, tf), _F32) + 2 * _nbytes((cap, tf), _BF16)
             + _nbytes((MOE_CHUNK_GATE_UP, d), _F32) + 6 * _nbytes((MOE_CHUNK_GATE_UP, tf), _F32))
    act = pl.pallas_call(
        _gmm_gate_up_body,
        grid_spec=pltpu.PrefetchScalarGridSpec(
            num_scalar_prefetch=4,
            grid=(n_sb, nf),
            in_specs=[
                pl.BlockSpec(memory_space=pl.ANY),
                pl.BlockSpec((None, None, d, tf), lambda s, j, e, b, r, nl: (layer, e[s], 0, col(s, j, nl, nf))),
                pl.BlockSpec((None, None, d, tf), lambda s, j, e, b, r, nl: (layer, e[s], 0, nf + col(s, j, nl, nf))),
                pl.BlockSpec((None, None, 1, tf), lambda s, j, e, b, r, nl: (layer, e[s], 0, col(s, j, nl, nf))),
                pl.BlockSpec((None, None, 1, tf), lambda s, j, e, b, r, nl: (layer, e[s], 0, nf + col(s, j, nl, nf))),
            ],
            out_specs=pl.BlockSpec((cap, tf), lambda s, j, e, b, r, nl: (b[s], col(s, j, nl, nf))),
            scratch_shapes=[pltpu.VMEM((cap, half), _U32), pltpu.SemaphoreType.DMA((cap // MOE_LOAD_ROWS,))],
        ),
        out_shape=jax.ShapeDtypeStruct((n_slots, d_ff), _BF16),
        compiler_params=_cparams(("arbitrary", "arbitrary"), vmem1 + (4 << 20)),
        name="moe_gate_up",
    )(sbe, blk, sb_rows, n_live, xs, w_gu, w_gu, b_gu4, b_gu4)

    tn = _tile(d, 512)
    nn = d // tn
    vmem2 = (_nbytes((cap, d_ff), _BF16) + 2 * _nbytes((d_ff, tn), _F32) + 2 * _nbytes((cap, tn), _F32)
             + 3 * _nbytes((MOE_CHUNK_DOWN, tn), _F32))
    ys = pl.pallas_call(
        _gmm_down_body,
        grid_spec=pltpu.PrefetchScalarGridSpec(
            num_scalar_prefetch=4,
            grid=(n_sb, nn),
            in_specs=[
                pl.BlockSpec(memory_space=pl.ANY),
                pl.BlockSpec((None, None, d_ff, tn), lambda s, j, e, b, r, nl: (layer, e[s], 0, col(s, j, nl, nn))),
                pl.BlockSpec((None, None, 1, tn), lambda s, j, e, b, r, nl: (layer, e[s], 0, col(s, j, nl, nn))),
            ],
            out_specs=pl.BlockSpec((cap, tn), lambda s, j, e, b, r, nl: (b[s], col(s, j, nl, nn))),
            scratch_shapes=[pltpu.VMEM((cap, d_ff), _BF16), pltpu.SemaphoreType.DMA((cap // MOE_LOAD_ROWS,))],
        ),
        out_shape=jax.ShapeDtypeStruct((n_slots, d), _F32),
        compiler_params=_cparams(("arbitrary", "arbitrary"), vmem2 + (4 << 20)),
        name="moe_down",
    )(sbe, blk, sb_rows, n_live, act, w_down, b_down.reshape(b_down.shape[0], n_exp, 1, d))
    return ys


def _combine_ln_body(dest_ref, dest_nxt_ref, gate_ref, h_ref, e_ref, g_ref, b_ref, ys_ref, o_ref, buf, sem,
                     *, alpha, n_steps):
    i = pl.program_id(0)
    tm = gate_ref.shape[0]
    slot = i % 2
    group = COMBINE_GROUP

    def row_copy(src_row, slot_, t, k):
        return pltpu.make_async_copy(ys_ref.at[pl.ds(src_row, 1)], buf.at[slot_, k, pl.ds(t, 1)],
                                     sem.at[slot_])

    def wait_slot(slot_):
        def wait(t, c):
            for k in range(TOP_K):
                row_copy(0, slot_, t, k).wait()
            return c
        lax.fori_loop(0, tm, wait, 0, unroll=DMA_WAIT_UNROLL)

    @pl.when(i == 0)
    def _():
        def first(t, c):
            for k in range(TOP_K):
                row_copy(dest_ref[t * TOP_K + k], 0, t, k).start()
            return c
        lax.fori_loop(0, tm, first, 0)

    wait_slot(slot)
    gamma, beta = g_ref[...], b_ref[...]

    def combine_group(gi, c):
        t0 = pl.multiple_of(gi * group, group)
        for j in range(group):
            for k in range(TOP_K):
                row_copy(dest_nxt_ref[(t0 + j) * TOP_K + k], 1 - slot, t0 + j, k).start()
        rows = pl.ds(t0, group)
        gate = gate_ref[rows, :]
        acc = alpha * h_ref[rows, :] + e_ref[rows, :]
        for k in range(TOP_K):
            acc = acc + gate[:, k:k + 1] * buf[slot, k, rows, :]
        o_ref[rows, :] = _layer_norm_rows(acc, gamma, beta)
        return c

    lax.fori_loop(0, tm // group, combine_group, 0)

    @pl.when(i == n_steps - 1)
    def _():
        wait_slot(1 - slot)


def _combine_ln(ys, dest_flat, gate, hf, e, g, b, alpha, row_off, rows):
    d = ys.shape[1]
    tm = _tile(rows, 128)
    n_steps = rows // tm
    off = row_off // tm
    assert row_off % tm == 0 and tm % COMBINE_GROUP == 0
    blk = _nbytes((tm, d), _F32)
    row = lambda i: (i + off, 0)
    return pl.pallas_call(
        functools.partial(_combine_ln_body, alpha=alpha, n_steps=n_steps),
        grid=(n_steps,),
        in_specs=[
            pl.BlockSpec((tm * TOP_K,), lambda i: (i + off,), memory_space=pltpu.SMEM),
            pl.BlockSpec((tm * TOP_K,), lambda i: (jnp.minimum(i + 1, n_steps - 1) + off,),
                         memory_space=pltpu.SMEM),
            pl.BlockSpec((tm, TOP_K), row),
            pl.BlockSpec((tm, d), row),
            pl.BlockSpec((tm, d), row),
            pl.BlockSpec((1, d), lambda i: (0, 0)),
            pl.BlockSpec((1, d), lambda i: (0, 0)),
            pl.BlockSpec(memory_space=pl.ANY),
        ],
        out_specs=pl.BlockSpec((tm, d), lambda i: (i, 0)),
        out_shape=jax.ShapeDtypeStruct((rows, d), _F32),
        scratch_shapes=[pltpu.VMEM((2, TOP_K, tm, d), _F32), pltpu.SemaphoreType.DMA((2,))],
        compiler_params=_cparams(("arbitrary",), (2 * TOP_K + 10) * blk + (4 << 20)),
        name="moe_combine_ln",
    )(dest_flat, dest_flat, gate, hf, e, g.reshape(1, d), b.reshape(1, d), ys)


def _moe_capacity(n_assign, n_exp):
    mean = -(-n_assign // n_exp)
    return -(-(mean + mean // 8) // MOE_LOAD_ROWS) * MOE_LOAD_ROWS


def _moe_plan(counts, cap, n_sb):
    n_exp = counts.shape[0]
    sb_per_e = (counts + cap - 1) // cap
    sb_end = jnp.cumsum(sb_per_e)
    sb_start = sb_end - sb_per_e
    n_live = sb_end[-1]
    sidx = jnp.arange(n_sb, dtype=_I32)
    e_of = jnp.minimum(jnp.sum(sidx[:, None] >= sb_end[None, :], axis=1), n_exp - 1).astype(_I32)
    of_e = e_of[:, None] == jnp.arange(n_exp, dtype=_I32)[None, :]
    local = sidx - jnp.sum(jnp.where(of_e, sb_start[None, :], 0), axis=1)
    cnt = jnp.sum(jnp.where(of_e, counts[None, :], 0), axis=1)
    live = sidx < n_live
    rows = jnp.where(live, jnp.clip(cnt - local * cap, 0, cap), 0).astype(_I32)
    last = jnp.maximum(n_live - 1, 0)
    e_last = jnp.sum(jnp.where(sidx == last, e_of, 0))
    sbe = jnp.where(live, e_of, e_last).astype(_I32)
    blk = jnp.where(live, sidx, last).astype(_I32)
    seg_start = (sb_start * cap).astype(_I32)
    meta = (sbe, blk, rows, n_live.reshape(1).astype(_I32))
    seg_pad = (seg_start + counts) // V7X_SUBLANES * V7X_SUBLANES
    return meta, seg_start, seg_pad.astype(_I32)


def _moe_experts(hp, top_idx, rank, counts, w_gu, b_gu, w_down, b_down, layer):
    n = hp.shape[0]
    n_exp = counts.shape[1]
    n_assign = n * TOP_K
    cap = _moe_capacity(n_assign, n_exp)
    n_sb = n_assign // cap + n_exp
    meta, seg_start, seg_pad = _moe_plan(counts[0], cap, n_sb)
    of_e = top_idx[:, :, None] == jnp.arange(n_exp, dtype=_I32)[None, None, :]
    dest = jnp.sum(jnp.where(of_e, seg_start[None, None, :], 0), axis=-1) + rank
    dest_flat = dest.reshape(-1).astype(_I32)
    xs = _dispatch(hp, dest_flat, seg_pad, n_sb * cap)
    ys = _expert_mlp(xs, meta, w_gu, b_gu, w_down, b_down, layer, cap)
    return ys, dest_flat


def kernel(x_prompt, x_sample, p_prompt, p_sample, ln_emb_g, ln_emb_b, w_in, b_in, rpb, sgu_ln_g, sgu_ln_b, sgu_w, sgu_b, w_pa, w_pb, w_o, ln1_g, ln1_b, w_router, b_router, w_gu, b_gu, w_down, b_down, w_pg, w_ple, ln2_g, ln2_b):
    depth = w_in.shape[0]
    d = x_prompt.shape[-1]
    ba, ta = x_prompt.shape[:2]
    bb, tb = x_sample.shape[:2]
    na_rows, nb_rows = ba * ta, bb * tb
    na_width = w_pa.shape[1]
    sgu_width = w_pb.shape[1]
    heads = na_width // NA_HEAD_DIM
    alpha = (2.0 * depth) ** 0.25
    qkv_cols = 3 * na_width

    xf, xb = _ln_emb(x_prompt.reshape(na_rows, d), x_sample.reshape(nb_rows, d), ln_emb_g, ln_emb_b)
    p_all = jnp.concatenate([p_prompt.reshape(depth, na_rows, -1), p_sample.reshape(depth, nb_rows, -1)], axis=1)

    for l in range(depth):
        zqkv = _in_proj(xb, w_in, b_in, l, 0, qkv_cols, _BF16)
        zrest = _in_proj(xb, w_in, b_in, l, qkv_cols, w_in.shape[2] - qkv_cols, _F32)
        bias_tab = _na_bias_table(rpb[l])
        na_out = jnp.concatenate([
            _neighbourhood_attention(zqkv, bias_tab, 0, ba, ta, heads),
            _neighbourhood_attention(zqkv, bias_tab, na_rows, bb, tb, heads)], axis=0)
        sgu_out = _spatial_gating(zrest, sgu_ln_g, sgu_ln_b, sgu_w, sgu_b, l, 0, sgu_width, sgu_width)
        merged = _merge(na_out, sgu_out, w_pa, w_pb, zrest, l, 2 * sgu_width, 2 * sgu_width + d)
        s1 = _out_proj(merged, w_o, xf, l, alpha)
        hf, hb, hp, top_idx, gate, rank, counts = _ln_route(s1, ln1_g[l], ln1_b[l], w_router, b_router, l)
        ys, dest_flat = _moe_experts(hp, top_idx, rank, counts, w_gu, b_gu, w_down, b_down, l)
        e = _ple_gate(hb, w_pg, p_all[l], w_ple, l)

        def layer_out(row_off, rows, l=l, ys=ys, dest_flat=dest_flat, gate=gate, hf=hf, e=e):
            return _combine_ln(ys, dest_flat, gate, hf, e, ln2_g[l], ln2_b[l], alpha, row_off, rows)

        if l + 1 < depth:
            xf = layer_out(0, na_rows + nb_rows)
            xb = xf.astype(_BF16)

    y_a = layer_out(0, na_rows)
    y_b = layer_out(na_rows, nb_rows)
    return (y_a.reshape(ba, ta, d), y_b.reshape(bb, tb, d))
```

```python
import functools

import jax
import jax.numpy as jnp
import numpy as np
from jax import lax
from jax.experimental import pallas as pl
from jax.experimental.pallas import tpu as pltpu

_F32 = jnp.float32
_BF16 = jnp.bfloat16
_I32 = jnp.int32
_U32 = jnp.uint32

LN_EPS = 1e-5
GRID_W = 64
WIN_R = 8
WIN_C = 16
NA_HEAD_DIM = 128
TOP_K = 4
SWIGLU_ALPHA = 1.702
SWIGLU_LIMIT = 7.0
MASK_VALUE = -1e30
NA_BLOCK_ROWS = 4
NA_KEY_ROWS = 12
NA_UNROLL = 4

V7X_VMEM_BYTES = 64 * 1024 * 1024
V7X_LANES = 128
V7X_SUBLANES = 8
V7X_MXU_DIM = 256
MOE_CHUNK_GATE_UP = 2 * V7X_MXU_DIM
MOE_CHUNK_DOWN = 4 * V7X_MXU_DIM
MOE_TAIL = V7X_MXU_DIM // 2
MOE_LOAD_ROWS = V7X_MXU_DIM
MOE_ZERO_ROWS = MOE_TAIL + V7X_SUBLANES
DMA_WAIT_UNROLL = 8
COMBINE_GROUP = 4 * V7X_SUBLANES


def _cparams(semantics, vmem_bytes):
    limit = min(int(vmem_bytes), V7X_VMEM_BYTES - 4 * 1024 * 1024)
    return pltpu.CompilerParams(dimension_semantics=semantics, vmem_limit_bytes=limit)


def _nbytes(shape, dtype):
    n = 1
    for s in shape:
        n *= s
    return n * jnp.dtype(dtype).itemsize


def _tile(dim, pref):
    t = min(dim, pref)
    while dim % t:
        t //= 2
    return t


def _layer_norm_rows(x, g, b):
    mu = jnp.mean(x, axis=-1, keepdims=True)
    xc = x - mu
    var = jnp.mean(xc * xc, axis=-1, keepdims=True)
    return xc * lax.rsqrt(var + LN_EPS) * g + b


def _gelu(x):
    return 0.5 * x * (1.0 + lax.erf(x * 0.7071067811865476))


def _sigmoid(x):
    return 1.0 / (1.0 + jnp.exp(-x))


def _ln_emb_body(xa_ref, xb_ref, g_ref, b_ref, of_ref, ob_ref, *, na):
    i = pl.program_id(0)

    def emit(x_ref):
        y = _layer_norm_rows(x_ref[...], g_ref[...], b_ref[...])
        of_ref[...] = y
        ob_ref[...] = y.astype(_BF16)

    @pl.when(i < na)
    def _():
        emit(xa_ref)

    @pl.when(i >= na)
    def _():
        emit(xb_ref)


def _ln_emb(xa, xb, g, b):
    d = xa.shape[1]
    tm = _tile(min(xa.shape[0], xb.shape[0]), 256)
    na, nb = xa.shape[0] // tm, xb.shape[0] // tm
    n = xa.shape[0] + xb.shape[0]
    blk = _nbytes((tm, d), _F32)
    return pl.pallas_call(
        functools.partial(_ln_emb_body, na=na),
        grid=(na + nb,),
        in_specs=[
            pl.BlockSpec((tm, d), lambda i: (jnp.minimum(i, na - 1), 0)),
            pl.BlockSpec((tm, d), lambda i: (jnp.maximum(i - na, 0), 0)),
            pl.BlockSpec((1, d), lambda i: (0, 0)),
            pl.BlockSpec((1, d), lambda i: (0, 0)),
        ],
        out_specs=[pl.BlockSpec((tm, d), lambda i: (i, 0)),
                   pl.BlockSpec((tm, d), lambda i: (i, 0))],
        out_shape=[jax.ShapeDtypeStruct((n, d), _F32), jax.ShapeDtypeStruct((n, d), _BF16)],
        compiler_params=_cparams(("arbitrary",), 12 * blk),
        name="ln_emb",
    )(xa, xb, g.reshape(1, d), b.reshape(1, d))


def _ln_route_body(s_ref, g_ref, b_ref, wr_hi_ref, wr_lo_ref, br_ref, hf_ref, hb_ref, hp_ref, idx_ref,
                   gate_ref, rank_ref, cnt_ref, carry_ref):
    y = _layer_norm_rows(s_ref[...], g_ref[...], b_ref[...])
    hf_ref[...] = y
    yb = y.astype(_BF16)
    hb_ref[...] = yb
    bits = lax.bitcast_convert_type(yb.astype(_F32), _U32)
    half = bits.shape[1] // 2
    hp_ref[...] = bits[:, :half] | (bits[:, half:] >> 16)

    @pl.when(pl.program_id(0) == 0)
    def _():
        carry_ref[...] = jnp.zeros_like(carry_ref)

    y_lo = (y - yb.astype(_F32)).astype(_BF16)
    logits = (jnp.dot(yb, wr_hi_ref[...], preferred_element_type=_F32)
              + jnp.dot(yb, wr_lo_ref[...], preferred_element_type=_F32)
              + jnp.dot(y_lo, wr_hi_ref[...], preferred_element_type=_F32) + br_ref[...])
    idx, gate, rank, carry = _route(logits, carry_ref[...])
    carry_ref[...] = carry
    idx_ref[...] = idx
    gate_ref[...] = gate
    rank_ref[...] = rank
    cnt_ref[...] = carry.astype(_I32)


def _ln_route(s, g, b, w_router, b_router, layer):
    n, d = s.shape
    n_exp = w_router.shape[2]
    tm = _tile(n, 256)
    blk = _nbytes((tm, d), _F32)
    row = lambda i: (i, 0)
    w_hi = w_router[layer].astype(_BF16)
    w_lo = (w_router[layer] - w_hi.astype(_F32)).astype(_BF16)
    return pl.pallas_call(
        _ln_route_body,
        grid=(n // tm,),
        in_specs=[pl.BlockSpec((tm, d), row),
                  pl.BlockSpec((1, d), lambda i: (0, 0)),
                  pl.BlockSpec((1, d), lambda i: (0, 0)),
                  pl.BlockSpec((d, n_exp), lambda i: (0, 0)),
                  pl.BlockSpec((d, n_exp), lambda i: (0, 0)),
                  pl.BlockSpec((None, 1, n_exp), lambda i: (layer, 0, 0))],
        out_specs=[pl.BlockSpec((tm, d), row),
                   pl.BlockSpec((tm, d), row),
                   pl.BlockSpec((tm, d // 2), row),
                   pl.BlockSpec((tm, TOP_K), row),
                   pl.BlockSpec((tm, TOP_K), row),
                   pl.BlockSpec((tm, TOP_K), row),
                   pl.BlockSpec((1, n_exp), lambda i: (0, 0))],
        out_shape=[jax.ShapeDtypeStruct((n, d), _F32),
                   jax.ShapeDtypeStruct((n, d), _BF16),
                   jax.ShapeDtypeStruct((n, d // 2), _U32),
                   jax.ShapeDtypeStruct((n, TOP_K), _I32),
                   jax.ShapeDtypeStruct((n, TOP_K), _F32),
                   jax.ShapeDtypeStruct((n, TOP_K), _I32),
                   jax.ShapeDtypeStruct((1, n_exp), _I32)],
        scratch_shapes=[pltpu.VMEM((1, n_exp), _F32)],
        compiler_params=_cparams(("arbitrary",), 14 * blk + 2 * _nbytes((d, V7X_LANES), _F32)),
        name="ln_route",
    )(s, g.reshape(1, d), b.reshape(1, d), w_hi, w_lo, b_router.reshape(b_router.shape[0], 1, n_exp))


def _in_proj_body(a_ref, w_ref, b_ref, o_ref):
    acc = jnp.dot(a_ref[...], w_ref[...], preferred_element_type=_F32)
    o_ref[...] = (acc + b_ref[...]).astype(o_ref.dtype)


def _in_proj(a, w, bias, layer, col_off, cols, out_dtype):
    m, k = a.shape
    bm = _tile(m, 2048)
    bn = _tile(cols, 512)
    joff = col_off // bn
    bias3 = bias.reshape(bias.shape[0], 1, bias.shape[1])
    vmem = (_nbytes((bm, k), a.dtype) + 2 * _nbytes((k, bn), w.dtype)
            + 2 * _nbytes((bm, bn), out_dtype) + 2 * _nbytes((bm, bn), _F32))
    return pl.pallas_call(
        _in_proj_body,
        grid=(m // bm, cols // bn),
        in_specs=[
            pl.BlockSpec((bm, k), lambda i, j: (i, 0), pipeline_mode=pl.Buffered(1)),
            pl.BlockSpec((None, k, bn), lambda i, j: (layer, 0, j + joff)),
            pl.BlockSpec((None, 1, bn), lambda i, j: (layer, 0, j + joff)),
        ],
        out_specs=pl.BlockSpec((bm, bn), lambda i, j: (i, j)),
        out_shape=jax.ShapeDtypeStruct((m, cols), out_dtype),
        compiler_params=_cparams(("arbitrary", "arbitrary"), vmem + (4 << 20)),
        name="in_proj",
    )(a, w, bias3)


def _merge_body(na_ref, sg_ref, wa_ref, wb_ref, ga_ref, gb_ref, o_ref):
    ya = jnp.dot(na_ref[...], wa_ref[...], preferred_element_type=_F32)
    yb = jnp.dot(sg_ref[...], wb_ref[...], preferred_element_type=_F32)
    merged = _sigmoid(ga_ref[...]) * ya + _sigmoid(gb_ref[...]) * yb
    o_ref[...] = merged.astype(o_ref.dtype)


def _merge(na_out, sgu_out, w_pa, w_pb, zrest, layer, ga_col, gb_col):
    m, ka = na_out.shape
    kb = sgu_out.shape[1]
    d = w_pa.shape[2]
    bm = _tile(m, 1024)
    bn = _tile(d, 512)
    ja, jb = ga_col // bn, gb_col // bn
    vmem = 2 * (_nbytes((bm, ka), _BF16) + _nbytes((bm, kb), _BF16) + _nbytes((ka, bn), _F32)
                + _nbytes((kb, bn), _F32) + 2 * _nbytes((bm, bn), _F32) + _nbytes((bm, bn), _BF16))
    vmem += 3 * _nbytes((bm, bn), _F32)
    return pl.pallas_call(
        _merge_body,
        grid=(m // bm, d // bn),
        in_specs=[
            pl.BlockSpec((bm, ka), lambda i, j: (i, 0)),
            pl.BlockSpec((bm, kb), lambda i, j: (i, 0)),
            pl.BlockSpec((None, ka, bn), lambda i, j: (layer, 0, j)),
            pl.BlockSpec((None, kb, bn), lambda i, j: (layer, 0, j)),
            pl.BlockSpec((bm, bn), lambda i, j: (i, j + ja)),
            pl.BlockSpec((bm, bn), lambda i, j: (i, j + jb)),
        ],
        out_specs=pl.BlockSpec((bm, bn), lambda i, j: (i, j)),
        out_shape=jax.ShapeDtypeStruct((m, d), _BF16),
        compiler_params=_cparams(("arbitrary", "arbitrary"), vmem + (4 << 20)),
        name="merge",
    )(na_out, sgu_out, w_pa, w_pb, zrest, zrest)


def _out_proj_body(a_ref, w_ref, x_ref, o_ref, *, alpha):
    acc = jnp.dot(a_ref[...], w_ref[...], preferred_element_type=_F32)
    o_ref[...] = alpha * x_ref[...] + acc


def _out_proj(a, w, resid, layer, alpha):
    m, k = a.shape
    d = w.shape[2]
    bm = _tile(m, 1024)
    bn = _tile(d, 512)
    vmem = 2 * (_nbytes((bm, k), a.dtype) + _nbytes((k, bn), w.dtype) + 2 * _nbytes((bm, bn), _F32))
    vmem += 2 * _nbytes((bm, bn), _F32)
    return pl.pallas_call(
        functools.partial(_out_proj_body, alpha=alpha),
        grid=(m // bm, d // bn),
        in_specs=[
            pl.BlockSpec((bm, k), lambda i, j: (i, 0)),
            pl.BlockSpec((None, k, bn), lambda i, j: (layer, 0, j)),
            pl.BlockSpec((bm, bn), lambda i, j: (i, j)),
        ],
        out_specs=pl.BlockSpec((bm, bn), lambda i, j: (i, j)),
        out_shape=jax.ShapeDtypeStruct((m, d), _F32),
        compiler_params=_cparams(("arbitrary", "arbitrary"), vmem + (4 << 20)),
        name="out_proj",
    )(a, w, resid)


def _ple_gate_body(a_ref, w_ref, pa_ref, pb_ref, wp_ref, o_ref, *, na):
    gate = _sigmoid(jnp.dot(a_ref[...], w_ref[...], preferred_element_type=_F32))
    p = jnp.where(pl.program_id(0) < na, pa_ref[...], pb_ref[...])
    ple = jnp.dot(p, wp_ref[...], preferred_element_type=_F32)
    o_ref[...] = gate * ple


def _ple_gate(hb, w_pg, p_a, p_b, w_ple, layer):
    m, k = hb.shape
    d = w_pg.shape[2]
    kp = p_a.shape[2]
    bm = _tile(min(p_a.shape[1], p_b.shape[1]), 2048)
    assert p_a.shape[1] % bm == 0 and p_b.shape[1] % bm == 0 and p_a.shape[1] + p_b.shape[1] == m
    na = p_a.shape[1] // bm
    bn = _tile(d, 512)
    vmem = (_nbytes((bm, k), hb.dtype) + 3 * _nbytes((bm, kp), _F32)
            + 2 * (_nbytes((k, bn), w_pg.dtype) + _nbytes((kp, bn), _F32) + _nbytes((bm, bn), _F32))
            + 3 * _nbytes((bm, bn), _F32))
    return pl.pallas_call(
        functools.partial(_ple_gate_body, na=na),
        grid=(m // bm, d // bn),
        in_specs=[
            pl.BlockSpec((bm, k), lambda i, j: (i, 0), pipeline_mode=pl.Buffered(1)),
            pl.BlockSpec((None, k, bn), lambda i, j: (layer, 0, j)),
            pl.BlockSpec((None, bm, kp), lambda i, j: (layer, jnp.minimum(i, na - 1), 0),
                         pipeline_mode=pl.Buffered(1)),
            pl.BlockSpec((None, bm, kp), lambda i, j: (layer, jnp.maximum(i - na, 0), 0),
                         pipeline_mode=pl.Buffered(1)),
            pl.BlockSpec((None, kp, bn), lambda i, j: (layer, 0, j)),
        ],
        out_specs=pl.BlockSpec((bm, bn), lambda i, j: (i, j)),
        out_shape=jax.ShapeDtypeStruct((m, d), _F32),
        compiler_params=_cparams(("arbitrary", "arbitrary"), vmem + (4 << 20)),
        name="ple_gate",
    )(hb, w_pg, p_a, p_b, w_ple)


def _na_bias_table(rpb_l):
    heads = rpb_l.shape[0]
    half = WIN_R // 2
    i = np.arange(NA_BLOCK_ROWS)
    win_off = np.stack([np.zeros_like(i), i, np.full_like(i, NA_KEY_ROWS - WIN_R)])
    win_d = np.stack([i, np.full_like(i, half), half + i])
    kr = np.arange(NA_KEY_ROWS)
    j = kr[None, None, :] - win_off[:, :, None]
    row_ok = (j >= 0) & (j < WIN_R)
    dr = j - win_d[:, :, None] + (WIN_R - 1)
    n_dr, n_dc = 2 * WIN_R - 1, 2 * WIN_C - 1
    row_sel = np.where(row_ok, dr, n_dr)[..., None] == np.arange(n_dr + 1)
    cols = np.arange(GRID_W)
    col_start = np.clip(cols - WIN_C // 2, 0, GRID_W - WIN_C)
    col_ok = (cols[None, :] >= col_start[:, None]) & (cols[None, :] < col_start[:, None] + WIN_C)
    dc = cols[None, :] - cols[:, None] + (WIN_C - 1)
    col_sel = np.where(col_ok, dc, n_dc)[..., None] == np.arange(n_dc + 1)
    values = jnp.pad(rpb_l.astype(_F32), ((0, 0), (0, 1), (0, 1)), constant_values=MASK_VALUE)
    tab = jnp.einsum("hab,pika,qcb->hpiqkc", values, row_sel.astype(np.float32),
                     col_sel.astype(np.float32), precision=lax.Precision.HIGHEST)
    return tab.reshape(heads, 3, NA_BLOCK_ROWS * GRID_W, NA_KEY_ROWS * GRID_W)


def _na_body(q_ref, k_ref, v_ref, bias_ref, o_ref, *, rows, scale):
    n_q = NA_BLOCK_ROWS * GRID_W
    n_k = NA_KEY_ROWS * GRID_W
    n_blocks = rows // NA_BLOCK_ROWS

    def block_fn(ib, carry):
        r0 = ib * NA_BLOCK_ROWS
        kb0 = jnp.clip(r0 - WIN_R // 2, 0, rows - NA_KEY_ROWS)
        pattern = jnp.where(ib == 0, 0, jnp.where(ib == n_blocks - 1, 2, 1))
        q0 = pl.multiple_of(r0 * GRID_W, n_q)
        k0 = pl.multiple_of(kb0 * GRID_W, NA_BLOCK_ROWS * GRID_W)
        q = q_ref[pl.ds(q0, n_q), :]
        kw = k_ref[pl.ds(k0, n_k), :]
        vw = v_ref[pl.ds(k0, n_k), :]
        s = lax.dot_general(q, kw, (((1,), (1,)), ((), ())), preferred_element_type=_F32)
        s = s * scale + bias_ref[pattern]
        m = jnp.max(s, axis=-1, keepdims=True)
        e = jnp.exp(s - m)
        p = e / jnp.sum(e, axis=-1, keepdims=True)
        o = jnp.dot(p.astype(_BF16), vw, preferred_element_type=_F32)
        o_ref[pl.ds(q0, n_q), :] = o.astype(o_ref.dtype)
        return carry

    lax.fori_loop(0, n_blocks, block_fn, 0, unroll=NA_UNROLL)


def _neighbourhood_attention(zqkv, bias_tab, row_off, batch, seq, heads):
    assert seq % GRID_W == 0 and row_off % seq == 0
    rows = seq // GRID_W
    assert rows % NA_BLOCK_ROWS == 0 and rows >= NA_KEY_ROWS
    boff = row_off // seq
    dh = NA_HEAD_DIM
    n_q = NA_BLOCK_ROWS * GRID_W
    n_k = NA_KEY_ROWS * GRID_W
    vmem = 2 * (4 * _nbytes((seq, dh), _BF16) + _nbytes((3, n_q, n_k), _F32)) + 6 * _nbytes((n_q, n_k), _F32)
    return pl.pallas_call(
        functools.partial(_na_body, rows=rows, scale=dh ** -0.5),
        grid=(heads, batch),
        in_specs=[
            pl.BlockSpec((seq, dh), lambda h, b: (b + boff, h)),
            pl.BlockSpec((seq, dh), lambda h, b: (b + boff, heads + h)),
            pl.BlockSpec((seq, dh), lambda h, b: (b + boff, 2 * heads + h)),
            pl.BlockSpec((None, 3, n_q, n_k), lambda h, b: (h, 0, 0, 0)),
        ],
        out_specs=pl.BlockSpec((seq, dh), lambda h, b: (b, h)),
        out_shape=jax.ShapeDtypeStruct((batch * seq, heads * dh), _BF16),
        compiler_params=_cparams(("arbitrary", "arbitrary"), vmem),
        name="neighbourhood_attention",
    )(zqkv, zqkv, zqkv, bias_tab)


def _sgu_body(u_ref, v_ref, g_ref, b_ref, ws_ref, bs_ref, o_ref, *, groups):
    u = _gelu(u_ref[...])
    vn = _layer_norm_rows(_gelu(v_ref[...]), g_ref[...], b_ref[...])
    gw = u.shape[1] // groups
    for g in range(groups):
        sl = slice(g * gw, (g + 1) * gw)
        s = jnp.dot(ws_ref[g], vn[:, sl], preferred_element_type=_F32) + bs_ref[g]
        o_ref[:, sl] = (u[:, sl] * s).astype(o_ref.dtype)


def _spatial_gating(zrest, ln_g, ln_b, w_s, b_s, layer, u_col, v_col, width):
    n = zrest.shape[0]
    groups, chunk = w_s.shape[1], w_s.shape[2]
    ju, jv = u_col // width, v_col // width
    blk = _nbytes((chunk, width), _F32)
    return pl.pallas_call(
        functools.partial(_sgu_body, groups=groups),
        grid=(n // chunk,),
        in_specs=[
            pl.BlockSpec((chunk, width), lambda i: (i, ju)),
            pl.BlockSpec((chunk, width), lambda i: (i, jv)),
            pl.BlockSpec((None, 1, width), lambda i: (layer, 0, 0)),
            pl.BlockSpec((None, 1, width), lambda i: (layer, 0, 0)),
            pl.BlockSpec((None, groups, chunk, chunk), lambda i: (layer, 0, 0, 0)),
            pl.BlockSpec((None, groups, chunk, 1), lambda i: (layer, 0, 0, 0)),
        ],
        out_specs=pl.BlockSpec((chunk, width), lambda i: (i, 0)),
        out_shape=jax.ShapeDtypeStruct((n, width), _BF16),
        compiler_params=_cparams(("arbitrary",), 12 * blk + (4 << 20)),
        name="spatial_gating",
    )(zrest, zrest, ln_g.reshape(-1, 1, width), ln_b.reshape(-1, 1, width), w_s,
      b_s.reshape(b_s.shape[0], groups, chunk, 1))


def _route(logits, carry):
    tm, n_exp = logits.shape
    e_iota = lax.broadcasted_iota(_I32, (tm, n_exp), 1)
    k_iota = lax.broadcasted_iota(_I32, (tm, TOP_K), 1)
    work = logits
    sels, vals = [], []
    idx_out = jnp.zeros((tm, TOP_K), _I32)
    for k in range(TOP_K):
        m = jnp.max(work, axis=-1, keepdims=True)
        idx = jnp.min(jnp.where(work == m, e_iota, n_exp), axis=-1, keepdims=True)
        sel = e_iota == idx
        sels.append(sel)
        vals.append(m)
        idx_out = jnp.where(k_iota == k, idx, idx_out)
        work = jnp.where(sel, -jnp.inf, work)
    exps = [jnp.exp(v - vals[0]) for v in vals]
    denom = exps[0]
    for e in exps[1:]:
        denom = denom + e
    gate_out = jnp.zeros((tm, TOP_K), _F32)
    for k in range(TOP_K):
        gate_out = jnp.where(k_iota == k, exps[k] / denom, gate_out)

    onehot = jnp.zeros((tm, n_exp), _F32)
    for sel in sels:
        onehot = onehot + sel.astype(_F32)
    r_iota = lax.broadcasted_iota(_I32, (tm, tm), 0)
    c_iota = lax.broadcasted_iota(_I32, (tm, tm), 1)
    lower = (c_iota < r_iota).astype(_BF16)
    pos = jnp.dot(lower, onehot.astype(_BF16), preferred_element_type=_F32) + carry
    rank_out = jnp.zeros((tm, TOP_K), _I32)
    for k in range(TOP_K):
        rk = jnp.sum(jnp.where(sels[k], pos, 0.0), axis=-1, keepdims=True).astype(_I32)
        rank_out = jnp.where(k_iota == k, rk, rank_out)
    return idx_out, gate_out, rank_out, carry + jnp.sum(onehot, axis=0, keepdims=True)


def _dispatch_body(pad_ref, dest_ref, hp_ref, xs_ref, zero_ref, sem):
    tm = hp_ref.shape[0]

    @pl.when(pl.program_id(0) == 0)
    def _():
        zero_ref[...] = jnp.zeros_like(zero_ref)
        n_zero = zero_ref.shape[0]
        copies = [pltpu.make_async_copy(
            zero_ref, xs_ref.at[pl.ds(pl.multiple_of(pad_ref[e], V7X_SUBLANES), n_zero)], sem)
            for e in range(pad_ref.shape[0])]
        for c in copies:
            c.start()
        for c in copies:
            c.wait()

    def row_copy(t, k):
        dst = xs_ref.at[pl.ds(dest_ref[t * TOP_K + k], 1)]
        return pltpu.make_async_copy(hp_ref.at[pl.ds(t, 1)], dst, sem)

    def start(t, c):
        for k in range(TOP_K):
            row_copy(t, k).start()
        return c

    def wait(t, c):
        for k in range(TOP_K):
            row_copy(t, k).wait()
        return c

    lax.fori_loop(0, tm, start, 0)
    lax.fori_loop(0, tm, wait, 0, unroll=DMA_WAIT_UNROLL)


def _dispatch(hp, dest_flat, seg_pad, n_slots):
    n, w = hp.shape
    tm = _tile(n, 256)
    return pl.pallas_call(
        _dispatch_body,
        grid_spec=pltpu.PrefetchScalarGridSpec(
            num_scalar_prefetch=1,
            grid=(n // tm,),
            in_specs=[
                pl.BlockSpec((tm * TOP_K,), lambda i, pd: (i,), memory_space=pltpu.SMEM),
                pl.BlockSpec((tm, w), lambda i, pd: (i, 0)),
            ],
            out_specs=pl.BlockSpec(memory_space=pl.ANY),
            scratch_shapes=[pltpu.VMEM((MOE_ZERO_ROWS, w), hp.dtype), pltpu.SemaphoreType.DMA(())],
        ),
        out_shape=jax.ShapeDtypeStruct((n_slots + MOE_ZERO_ROWS, w), hp.dtype),
        compiler_params=_cparams(("arbitrary",), 4 * _nbytes((tm, w), hp.dtype) + (4 << 20)),
        name="moe_dispatch",
    )(seg_pad, dest_flat, hp)


def _for_each_piece(rows_p, chunk, fn):
    base = rows_p // chunk * chunk
    rem = rows_p - base
    size = chunk // 2
    while size >= MOE_TAIL:
        def piece(size=size):
            off = base + (rem - rem % (2 * size))
            fn(pl.multiple_of(off, size), size)
        pl.when((rem // size) % 2 == 1)(piece)
        size //= 2


def _for_each_chunk(rows, chunk, fn, ready, per_trip=1):
    rows_p = ((rows + (MOE_TAIL - 1)) // MOE_TAIL) * MOE_TAIL
    span = chunk * per_trip

    def full(c, carry):
        ready(pl.multiple_of(c * span, span), span)
        for i in range(per_trip):
            fn(pl.multiple_of(c * span + i * chunk, chunk), chunk)
        return carry

    def piece(r0, size):
        ready(r0, size)
        fn(r0, size)

    lax.fori_loop(0, rows_p // span, full, 0)
    _for_each_piece(rows_p, span, piece)


def _unpack_halves(words):
    lo_cols = lax.bitcast_convert_type(words & jnp.uint32(0xFFFF0000), _F32)
    hi_cols = lax.bitcast_convert_type(words << 16, _F32)
    return lo_cols, hi_cols


def _superblock_loader(src_ref, buf_ref, sem, row0, rows_p):
    n_pieces = buf_ref.shape[0] // MOE_LOAD_ROWS

    def piece(c):
        src = src_ref.at[pl.ds(pl.multiple_of(row0 + c * MOE_LOAD_ROWS, MOE_LOAD_ROWS), MOE_LOAD_ROWS)]
        dst = buf_ref.at[pl.ds(pl.multiple_of(c * MOE_LOAD_ROWS, MOE_LOAD_ROWS), MOE_LOAD_ROWS)]
        return pltpu.make_async_copy(src, dst, sem.at[c])

    def start_all():
        for c in range(n_pieces):
            pl.when(c * MOE_LOAD_ROWS < rows_p)(lambda c=c: piece(c).start())

    def wait_rows(r0, size):
        for i in range(max(size // MOE_LOAD_ROWS, 1)):
            piece(r0 // MOE_LOAD_ROWS + i).wait()

    return start_all, wait_rows


def _gmm_gate_up_body(sbe_ref, blk_ref, rows_ref, nlive_ref, xs_ref, wg_ref, wu_ref, bg_ref, bu_ref, o_ref,
                      x_ref, sem):
    del sbe_ref, nlive_ref
    s = pl.program_id(0)
    first_tile = pl.program_id(1) == 0
    half = x_ref.shape[1]
    rows_p = ((rows_ref[s] + (MOE_TAIL - 1)) // MOE_TAIL) * MOE_TAIL
    start_all, wait_rows = _superblock_loader(xs_ref, x_ref, sem, blk_ref[s] * x_ref.shape[0], rows_p)
    pl.when(first_tile)(start_all)

    def ready(r0, size):
        pl.when(first_tile)(lambda: wait_rows(r0, size))

    def chunk(r0, size):
        xa, xb = _unpack_halves(x_ref[pl.ds(r0, size), :])
        g = (jnp.dot(xa, wg_ref[:half, :], preferred_element_type=_F32)
             + jnp.dot(xb, wg_ref[half:, :], preferred_element_type=_F32) + bg_ref[...])
        u = (jnp.dot(xa, wu_ref[:half, :], preferred_element_type=_F32)
             + jnp.dot(xb, wu_ref[half:, :], preferred_element_type=_F32) + bu_ref[...])
        g = jnp.minimum(g, SWIGLU_LIMIT)
        u = jnp.clip(u, -SWIGLU_LIMIT, SWIGLU_LIMIT)
        act = (u + 1.0) * (g * _sigmoid(g * SWIGLU_ALPHA))
        o_ref[pl.ds(r0, size), :] = act.astype(o_ref.dtype)

    _for_each_chunk(rows_ref[s], MOE_CHUNK_GATE_UP, chunk, ready, per_trip=2)


def _gmm_down_body(sbe_ref, blk_ref, rows_ref, nlive_ref, act_ref, w_ref, b_ref, o_ref, a_ref, sem):
    del sbe_ref, nlive_ref
    s = pl.program_id(0)
    first_tile = pl.program_id(1) == 0
    rows_p = ((rows_ref[s] + (MOE_TAIL - 1)) // MOE_TAIL) * MOE_TAIL
    start_all, wait_rows = _superblock_loader(act_ref, a_ref, sem, blk_ref[s] * a_ref.shape[0], rows_p)
    pl.when(first_tile)(start_all)

    def ready(r0, size):
        pl.when(first_tile)(lambda: wait_rows(r0, size))

    def chunk(r0, size):
        y = jnp.dot(a_ref[pl.ds(r0, size), :], w_ref[...], preferred_element_type=_F32) + b_ref[...]
        o_ref[pl.ds(r0, size), :] = y

    _for_each_chunk(rows_ref[s], MOE_CHUNK_DOWN, chunk, ready, per_trip=2)


def _expert_mlp(xs, meta, w_gu, b_gu, w_down, b_down, layer, cap):
    sbe, blk, sb_rows, n_live = meta
    n_sb = sbe.shape[0]
    n_slots, half = xs.shape
    d = 2 * half
    n_exp = w_gu.shape[1]
    d_ff = w_gu.shape[3] // 2
    tf = _tile(d_ff, 256)
    nf = d_ff // tf
    b_gu4 = b_gu.reshape(b_gu.shape[0], n_exp, 1, 2 * d_ff)

    def col(s, j, nl, n_tiles):
        return jnp.where(s < nl[0], j, n_tiles - 1)

    vmem1 = (_nbytes((cap, half), _U32) + 4 * _nbytes((d, tf), _F32) + 2 * _nbytes((cap, tf), _BF16)
             + _nbytes((MOE_CHUNK_GATE_UP, d), _F32) + 6 * _nbytes((MOE_CHUNK_GATE_UP, tf), _F32))
    act = pl.pallas_call(
        _gmm_gate_up_body,
        grid_spec=pltpu.PrefetchScalarGridSpec(
            num_scalar_prefetch=4,
            grid=(n_sb, nf),
            in_specs=[
                pl.BlockSpec(memory_space=pl.ANY),
                pl.BlockSpec((None, None, d, tf), lambda s, j, e, b, r, nl: (layer, e[s], 0, col(s, j, nl, nf))),
                pl.BlockSpec((None, None, d, tf), lambda s, j, e, b, r, nl: (layer, e[s], 0, nf + col(s, j, nl, nf))),
                pl.BlockSpec((None, None, 1, tf), lambda s, j, e, b, r, nl: (layer, e[s], 0, col(s, j, nl, nf))),
                pl.BlockSpec((None, None, 1, tf), lambda s, j, e, b, r, nl: (layer, e[s], 0, nf + col(s, j, nl, nf))),
            ],
            out_specs=pl.BlockSpec((cap, tf), lambda s, j, e, b, r, nl: (b[s], col(s, j, nl, nf))),
            scratch_shapes=[pltpu.VMEM((cap, half), _U32), pltpu.SemaphoreType.DMA((cap // MOE_LOAD_ROWS,))],
        ),
        out_shape=jax.ShapeDtypeStruct((n_slots, d_ff), _BF16),
        compiler_params=_cparams(("arbitrary", "arbitrary"), vmem1 + (4 << 20)),
        name="moe_gate_up",
    )(sbe, blk, sb_rows, n_live, xs, w_gu, w_gu, b_gu4, b_gu4)

    tn = _tile(d, 512)
    nn = d // tn
    vmem2 = (_nbytes((cap, d_ff), _BF16) + 2 * _nbytes((d_ff, tn), _F32) + 2 * _nbytes((cap, tn), _F32)
             + 3 * _nbytes((MOE_CHUNK_DOWN, tn), _F32))
    ys = pl.pallas_call(
        _gmm_down_body,
        grid_spec=pltpu.PrefetchScalarGridSpec(
            num_scalar_prefetch=4,
            grid=(n_sb, nn),
            in_specs=[
                pl.BlockSpec(memory_space=pl.ANY),
                pl.BlockSpec((None, None, d_ff, tn), lambda s, j, e, b, r, nl: (layer, e[s], 0, col(s, j, nl, nn))),
                pl.BlockSpec((None, None, 1, tn), lambda s, j, e, b, r, nl: (layer, e[s], 0, col(s, j, nl, nn))),
            ],
            out_specs=pl.BlockSpec((cap, tn), lambda s, j, e, b, r, nl: (b[s], col(s, j, nl, nn))),
            scratch_shapes=[pltpu.VMEM((cap, d_ff), _BF16), pltpu.SemaphoreType.DMA((cap // MOE_LOAD_ROWS,))],
        ),
        out_shape=jax.ShapeDtypeStruct((n_slots, d), _F32),
        compiler_params=_cparams(("arbitrary", "arbitrary"), vmem2 + (4 << 20)),
        name="moe_down",
    )(sbe, blk, sb_rows, n_live, act, w_down, b_down.reshape(b_down.shape[0], n_exp, 1, d))
    return ys


def _combine_ln_body(dest_ref, dest_nxt_ref, gate_ref, h_ref, e_ref, g_ref, b_ref, ys_ref, o_ref, buf, sem,
                     *, alpha, n_steps):
    i = pl.program_id(0)
    tm = gate_ref.shape[0]
    slot = i % 2
    group = COMBINE_GROUP

    def row_copy(src_row, slot_, t, k):
        return pltpu.make_async_copy(ys_ref.at[pl.ds(src_row, 1)], buf.at[slot_, k, pl.ds(t, 1)],
                                     sem.at[slot_])

    def wait_slot(slot_):
        def wait(t, c):
            for k in range(TOP_K):
                row_copy(0, slot_, t, k).wait()
            return c
        lax.fori_loop(0, tm, wait, 0, unroll=DMA_WAIT_UNROLL)

    @pl.when(i == 0)
    def _():
        def first(t, c):
            for k in range(TOP_K):
                row_copy(dest_ref[t * TOP_K + k], 0, t, k).start()
            return c
        lax.fori_loop(0, tm, first, 0)

    wait_slot(slot)
    gamma, beta = g_ref[...], b_ref[...]

    def combine_group(gi, c):
        t0 = pl.multiple_of(gi * group, group)
        for j in range(group):
            for k in range(TOP_K):
                row_copy(dest_nxt_ref[(t0 + j) * TOP_K + k], 1 - slot, t0 + j, k).start()
        rows = pl.ds(t0, group)
        gate = gate_ref[rows, :]
        acc = alpha * h_ref[rows, :] + e_ref[rows, :]
        for k in range(TOP_K):
            acc = acc + gate[:, k:k + 1] * buf[slot, k, rows, :]
        o_ref[rows, :] = _layer_norm_rows(acc, gamma, beta)
        return c

    lax.fori_loop(0, tm // group, combine_group, 0)

    @pl.when(i == n_steps - 1)
    def _():
        wait_slot(1 - slot)


def _combine_ln(ys, dest_flat, gate, hf, e, g, b, alpha, row_off, rows):
    d = ys.shape[1]
    tm = _tile(rows, 128)
    n_steps = rows // tm
    off = row_off // tm
    assert row_off % tm == 0 and tm % COMBINE_GROUP == 0
    blk = _nbytes((tm, d), _F32)
    row = lambda i: (i + off, 0)
    return pl.pallas_call(
        functools.partial(_combine_ln_body, alpha=alpha, n_steps=n_steps),
        grid=(n_steps,),
        in_specs=[
            pl.BlockSpec((tm * TOP_K,), lambda i: (i + off,), memory_space=pltpu.SMEM),
            pl.BlockSpec((tm * TOP_K,), lambda i: (jnp.minimum(i + 1, n_steps - 1) + off,),
                         memory_space=pltpu.SMEM),
            pl.BlockSpec((tm, TOP_K), row),
            pl.BlockSpec((tm, d), row),
            pl.BlockSpec((tm, d), row),
            pl.BlockSpec((1, d), lambda i: (0, 0)),
            pl.BlockSpec((1, d), lambda i: (0, 0)),
            pl.BlockSpec(memory_space=pl.ANY),
        ],
        out_specs=pl.BlockSpec((tm, d), lambda i: (i, 0)),
        out_shape=jax.ShapeDtypeStruct((rows, d), _F32),
        scratch_shapes=[pltpu.VMEM((2, TOP_K, tm, d), _F32), pltpu.SemaphoreType.DMA((2,))],
        compiler_params=_cparams(("arbitrary",), (2 * TOP_K + 10) * blk + (4 << 20)),
        name="moe_combine_ln",
    )(dest_flat, dest_flat, gate, hf, e, g.reshape(1, d), b.reshape(1, d), ys)


def _moe_capacity(n_assign, n_exp):
    mean = -(-n_assign // n_exp)
    return -(-(mean + mean // 8) // MOE_LOAD_ROWS) * MOE_LOAD_ROWS


def _moe_plan(counts, cap, n_sb):
    n_exp = counts.shape[0]
    sb_per_e = (counts + cap - 1) // cap
    sb_end = jnp.cumsum(sb_per_e)
    sb_start = sb_end - sb_per_e
    n_live = sb_end[-1]
    sidx = jnp.arange(n_sb, dtype=_I32)
    e_of = jnp.minimum(jnp.sum(sidx[:, None] >= sb_end[None, :], axis=1), n_exp - 1).astype(_I32)
    of_e = e_of[:, None] == jnp.arange(n_exp, dtype=_I32)[None, :]
    local = sidx - jnp.sum(jnp.where(of_e, sb_start[None, :], 0), axis=1)
    cnt = jnp.sum(jnp.where(of_e, counts[None, :], 0), axis=1)
    live = sidx < n_live
    rows = jnp.where(live, jnp.clip(cnt - local * cap, 0, cap), 0).astype(_I32)
    last = jnp.maximum(n_live - 1, 0)
    e_last = jnp.sum(jnp.where(sidx == last, e_of, 0))
    sbe = jnp.where(live, e_of, e_last).astype(_I32)
    blk = jnp.where(live, sidx, last).astype(_I32)
    seg_start = (sb_start * cap).astype(_I32)
    meta = (sbe, blk, rows, n_live.reshape(1).astype(_I32))
    seg_pad = (seg_start + counts) // V7X_SUBLANES * V7X_SUBLANES
    return meta, seg_start, seg_pad.astype(_I32)


def _moe_experts(hp, top_idx, rank, counts, w_gu, b_gu, w_down, b_down, layer):
    n = hp.shape[0]
    n_exp = counts.shape[1]
    n_assign = n * TOP_K
    cap = _moe_capacity(n_assign, n_exp)
    n_sb = n_assign // cap + n_exp
    meta, seg_start, seg_pad = _moe_plan(counts[0], cap, n_sb)
    of_e = top_idx[:, :, None] == jnp.arange(n_exp, dtype=_I32)[None, None, :]
    dest = jnp.sum(jnp.where(of_e, seg_start[None, None, :], 0), axis=-1) + rank
    dest_flat = dest.reshape(-1).astype(_I32)
    xs = _dispatch(hp, dest_flat, seg_pad, n_sb * cap)
    ys = _expert_mlp(xs, meta, w_gu, b_gu, w_down, b_down, layer, cap)
    return ys, dest_flat


def kernel(x_prompt, x_sample, p_prompt, p_sample, ln_emb_g, ln_emb_b, w_in, b_in, rpb, sgu_ln_g, sgu_ln_b, sgu_w, sgu_b, w_pa, w_pb, w_o, ln1_g, ln1_b, w_router, b_router, w_gu, b_gu, w_down, b_down, w_pg, w_ple, ln2_g, ln2_b):
    depth = w_in.shape[0]
    d = x_prompt.shape[-1]
    ba, ta = x_prompt.shape[:2]
    bb, tb = x_sample.shape[:2]
    na_rows, nb_rows = ba * ta, bb * tb
    na_width = w_pa.shape[1]
    sgu_width = w_pb.shape[1]
    heads = na_width // NA_HEAD_DIM
    alpha = (2.0 * depth) ** 0.25
    qkv_cols = 3 * na_width

    xf, xb = _ln_emb(x_prompt.reshape(na_rows, d), x_sample.reshape(nb_rows, d), ln_emb_g, ln_emb_b)
    p_a, p_b = p_prompt.reshape(depth, na_rows, -1), p_sample.reshape(depth, nb_rows, -1)

    for l in range(depth):
        zqkv = _in_proj(xb, w_in, b_in, l, 0, qkv_cols, _BF16)
        zrest = _in_proj(xb, w_in, b_in, l, qkv_cols, w_in.shape[2] - qkv_cols, _F32)
        bias_tab = _na_bias_table(rpb[l])
        na_out = jnp.concatenate([
            _neighbourhood_attention(zqkv, bias_tab, 0, ba, ta, heads),
            _neighbourhood_attention(zqkv, bias_tab, na_rows, bb, tb, heads)], axis=0)
        sgu_out = _spatial_gating(zrest, sgu_ln_g, sgu_ln_b, sgu_w, sgu_b, l, 0, sgu_width, sgu_width)
        merged = _merge(na_out, sgu_out, w_pa, w_pb, zrest, l, 2 * sgu_width, 2 * sgu_width + d)
        s1 = _out_proj(merged, w_o, xf, l, alpha)
        hf, hb, hp, top_idx, gate, rank, counts = _ln_route(s1, ln1_g[l], ln1_b[l], w_router, b_router, l)
        ys, dest_flat = _moe_experts(hp, top_idx, rank, counts, w_gu, b_gu, w_down, b_down, l)
        e = _ple_gate(hb, w_pg, p_a, p_b, w_ple, l)

        def layer_out(row_off, rows, l=l, ys=ys, dest_flat=dest_flat, gate=gate, hf=hf, e=e):
            return _combine_ln(ys, dest_flat, gate, hf, e, ln2_g[l], ln2_b[l], alpha, row_off, rows)

        if l + 1 < depth:
            xf = layer_out(0, na_rows + nb_rows)
            xb = xf.astype(_BF16)

    y_a = layer_out(0, na_rows)
    y_b = layer_out(na_rows, nb_rows)
    return (y_a.reshape(ba, ta, d), y_b.reshape(bb, tb, d))
```
